```python
import math
import jax, jax.numpy as jnp
from jax import lax
import numpy as np

D_MODEL = 1024
BATCH = 8
SEQ = 4096
DEPTH = 2

N_META = 16
EPS = 1e-6
D_FF = 4 * D_MODEL

S5_WIDTH = D_MODEL // 4
S5_GROUP = 16
S5_GROUPS = S5_WIDTH // S5_GROUP
S5_STATE = 64

SSD_WIDTH = D_MODEL // 2
SSD_HEAD_DIM = 64
SSD_HEADS = SSD_WIDTH // SSD_HEAD_DIM
SSD_GROUPS = 2
SSD_HEADS_PER_GROUP = SSD_HEADS // SSD_GROUPS
SSD_STATE = 128
SSD_CONV = 5
SSD_CHUNK = 128
SSD_CONV_CH = SSD_WIDTH + 2 * SSD_GROUPS * SSD_STATE

RWKV_WIDTH = D_MODEL // 4
RWKV_HEAD = 64
RWKV_HEADS = RWKV_WIDTH // RWKV_HEAD
RWKV_DECAY_LORA = 64
RWKV_AAA_LORA = 64
RWKV_GATE_LORA = 128
RWKV_LN_EPS = 64e-5

N_BRANCH = 3
IN_SPLITS = (S5_WIDTH, SSD_WIDTH, SSD_CONV_CH, SSD_HEADS, 4 * RWKV_WIDTH, N_BRANCH * D_MODEL)
N_IN = sum(IN_SPLITS)

kernel_name = 'hybrid_s5_ssd_rwkv7_bidir_encoder'


def rms_norm(x, w):
    xf = x.astype(jnp.float32)
    y = xf * lax.rsqrt(jnp.mean(xf * xf, -1, keepdims=True) + EPS)
    return (y * w).astype(x.dtype)


def _complex_affine(e1, e2):
    ar1, ai1, br1, bi1 = e1
    ar2, ai2, br2, bi2 = e2
    return (ar2 * ar1 - ai2 * ai1,
            ar2 * ai1 + ai2 * ar1,
            ar2 * br1 - ai2 * bi1 + br2,
            ar2 * bi1 + ai2 * br1 + bi2)


def s5_mixer(u, lam_re, lam_im, log_step, b_re, b_im, c_re, c_im, d_skip, glu_w, glu_b):
    bsz, length, _ = u.shape
    ug = u.reshape(bsz, length, S5_GROUPS, S5_GROUP)
    bu_re = jnp.einsum('gnh,blgh->blgn', b_re, ug)
    bu_im = jnp.einsum('gnh,blgh->blgn', b_im, ug)
    h_re = 0.0
    h_im = 0.0
    for direction in range(2):
        lr, li = lam_re[direction], lam_im[direction]
        step = jnp.exp(log_step[direction])[:, None]
        mag = jnp.exp(lr * step)
        ab_re, ab_im = mag * jnp.cos(li * step), mag * jnp.sin(li * step)
        den = lr * lr + li * li
        co_re = ((ab_re - 1.0) * lr + ab_im * li) / den
        co_im = (ab_im * lr - (ab_re - 1.0) * li) / den
        e_re = co_re * bu_re - co_im * bu_im
        e_im = co_re * bu_im + co_im * bu_re
        shape = (1, length) + ab_re.shape
        elems = (jnp.broadcast_to(ab_re, shape), jnp.broadcast_to(ab_im, shape), e_re, e_im)
        _, _, s_re, s_im = lax.associative_scan(_complex_affine, elems, reverse=(direction == 1), axis=1)
        h_re = h_re + s_re
        h_im = h_im + s_im
    y = jnp.einsum('ghn,blgn->blgh', c_re, h_re) - jnp.einsum('ghn,blgn->blgh', c_im, h_im)
    y = y.reshape(bsz, length, S5_WIDTH) + d_skip * u
    zz = jax.nn.gelu(y) @ glu_w + glu_b
    return zz[..., :S5_WIDTH] * jax.nn.sigmoid(zz[..., S5_WIDTH:])


def _segsum_exp(a_cs):
    q = a_cs.shape[2]
    diff = a_cs[:, :, :, None] - a_cs[:, :, None, :]
    mask = jnp.tril(jnp.ones((q, q), bool))[None, None, :, :, None, None]
    return jnp.exp(jnp.where(mask, diff, -jnp.inf))


def ssd_chunked(xdt, a_step, bmat, cmat):
    bsz, t = xdt.shape[:2]
    nc = t // SSD_CHUNK
    chunk = lambda z: z.reshape((bsz, nc, SSD_CHUNK) + z.shape[2:])
    xdt, a_step, bmat, cmat = chunk(xdt), chunk(a_step), chunk(bmat), chunk(cmat)
    a_cs = jnp.cumsum(a_step, axis=2)
    cb = jnp.einsum('bclgn,bcsgn->bclsg', cmat, bmat)
    m = cb[..., None] * _segsum_exp(a_cs)
    y_diag = jnp.einsum('bclsgj,bcsgjp->bclgjp', m, xdt)
    decay_states = jnp.exp(a_cs[:, :, -1:] - a_cs)
    states = jnp.einsum('bcsgn,bcsgjp->bcgjpn', bmat, xdt * decay_states[..., None])
    chunk_decay = jnp.exp(a_cs[:, :, -1])

    def step(h, inp):
        s, dec = inp
        return h * dec[..., None, None] + s, h

    h0 = jnp.zeros_like(states[:, 0])
    _, h_in = lax.scan(step, h0, (jnp.moveaxis(states, 1, 0), jnp.moveaxis(chunk_decay, 1, 0)))
    h_in = jnp.moveaxis(h_in, 0, 1)
    y_off = jnp.einsum('bclgn,bcgjpn->bclgjp', cmat, h_in) * jnp.exp(a_cs)[..., None]
    y = y_diag + y_off
    return y.reshape((bsz, t) + y.shape[3:])


def ssd_mixer(z, xbc, dt_raw, conv_w, conv_b, a_log, dt_bias, d_skip, norm_w):
    bsz, length, _ = xbc.shape
    pad = SSD_CONV // 2
    xbc = lax.conv_general_dilated(xbc, conv_w[:, None, :], window_strides=(1,), padding=[(pad, pad)],
                                   dimension_numbers=('NWC', 'WIO', 'NWC'),
                                   feature_group_count=SSD_CONV_CH) + conv_b
    xbc = jax.nn.silu(xbc)
    xs, bmat, cmat = jnp.split(xbc, [SSD_WIDTH, SSD_WIDTH + SSD_GROUPS * SSD_STATE], axis=-1)
    xh = xs.reshape(bsz, length, SSD_GROUPS, SSD_HEADS_PER_GROUP, SSD_HEAD_DIM)
    bmat = bmat.reshape(bsz, length, SSD_GROUPS, SSD_STATE)
    cmat = cmat.reshape(bsz, length, SSD_GROUPS, SSD_STATE)
    front = SSD_CHUNK - N_META
    padt = lambda t: jnp.pad(t, ((0, 0), (front, 0)) + ((0, 0),) * (t.ndim - 2))
    xh_p, b_p, c_p = padt(xh), padt(bmat), padt(cmat)
    y = 0.0
    for direction in range(2):
        dt = jax.nn.softplus(dt_raw + dt_bias[direction]).reshape(bsz, length, SSD_GROUPS, SSD_HEADS_PER_GROUP)
        dt_p = padt(dt)
        a_step = dt_p * (-jnp.exp(a_log[direction])).reshape(SSD_GROUPS, SSD_HEADS_PER_GROUP)
        xdt = xh_p * dt_p[..., None]
        if direction == 0:
            y = y + ssd_chunked(xdt, a_step, b_p, c_p)
        else:
            yb = ssd_chunked(jnp.flip(xdt, 1), jnp.flip(a_step, 1), jnp.flip(b_p, 1), jnp.flip(c_p, 1))
            y = y + jnp.flip(yb, 1)
    y = y[:, front:] + d_skip.reshape(SSD_GROUPS, SSD_HEADS_PER_GROUP)[..., None] * xh
    y = y.reshape(bsz, length, SSD_WIDTH) * jax.nn.silu(z)
    return rms_norm(y, norm_w)


def centred_shift(x):
    prev = jnp.pad(x[:, :-1], ((0, 0), (1, 0), (0, 0)))
    nxt = jnp.pad(x[:, 1:], ((0, 0), (0, 1), (0, 0)))
    return 0.5 * (prev + nxt) - x


def wkv7_scan(r, w, k, v, a, b, reverse):
    bsz, _, heads, n = r.shape
    xs = tuple(jnp.moveaxis(t, 1, 0) for t in (r, w, k, v, a, b))

    def step(state, inp):
        rt, wt, kt, vt, at, bt = inp
        sa = jnp.einsum('bhvk,bhk->bhv', state, at)
        state = state * wt[:, :, None, :] + sa[..., None] * bt[:, :, None, :] + vt[..., None] * kt[:, :, None, :]
        return state, jnp.einsum('bhvk,bhk->bhv', state, rt)

    s0 = jnp.zeros((bsz, heads, n, n), r.dtype)
    _, ys = lax.scan(step, s0, xs, reverse=reverse)
    return jnp.moveaxis(ys, 0, 1)


def rwkv7_mixer(r, k, v, xc, mu_rkv, mu_wag, w0, w1, w2, a0, a1, a2, g1, g2, k_k, k_a, r_k, ln_w, ln_b):
    bsz, length, _ = r.shape
    heads = lambda t: t.reshape(bsz, length, RWKV_HEADS, RWKV_HEAD)
    r = r + centred_shift(r) * mu_rkv[0]
    k = k + centred_shift(k) * mu_rkv[1]
    v = v + centred_shift(v) * mu_rkv[2]
    dxc = centred_shift(xc)
    xw = xc + dxc * mu_wag[0]
    xa = xc + dxc * mu_wag[1]
    xg = xc + dxc * mu_wag[2]
    g = jax.nn.sigmoid(xg @ g1) @ g2
    kk = heads(k * k_k)
    kk = kk * lax.rsqrt(jnp.sum(jnp.square(kk), -1, keepdims=True) + 1e-12)
    rh, vh = heads(r), heads(v)
    y = 0.0
    for direction in range(2):
        w_log = -jax.nn.softplus(-(w0[direction] + jnp.tanh(xw @ w1[direction]) @ w2[direction])) - 0.5
        decay = jnp.exp(-jnp.exp(w_log))
        a = jax.nn.sigmoid(a0[direction] + (xa @ a1[direction]) @ a2[direction])
        kd = k * (1.0 + (a - 1.0) * k_a)
        y = y + wkv7_scan(rh, heads(decay), heads(kd), vh, -kk, kk * heads(a), reverse=(direction == 1))
    yf = y.astype(jnp.float32)
    mean = jnp.mean(yf, -1, keepdims=True)
    var = jnp.mean(jnp.square(yf - mean), -1, keepdims=True)
    yn = ((yf - mean) * lax.rsqrt(var + RWKV_LN_EPS)).astype(y.dtype).reshape(bsz, length, RWKV_WIDTH)
    yn = yn * ln_w + ln_b
    bonus = jnp.sum(rh * heads(k) * r_k, -1, keepdims=True) * vh
    return (yn + bonus.reshape(bsz, length, RWKV_WIDTH)) * g


def setup_inputs(seed: int = 0) -> dict:
    key = jax.random.key(seed)
    ks = iter(jax.random.split(key, 64))

    def nrm(shape, scale):
        return scale * jax.random.normal(next(ks), shape, jnp.float32)

    def near(shape, centre, spread):
        return centre + spread * jax.random.normal(next(ks), shape, jnp.float32)

    def unif(shape, lo, hi):
        return jax.random.uniform(next(ks), shape, jnp.float32, lo, hi)

    x = nrm((BATCH, SEQ, D_MODEL), 1.0)
    meta_tokens = nrm((N_META, D_MODEL), 1.0)
    final_norm_w = near((D_MODEL,), 1.0, 0.02)
    mix_norm_w = near((DEPTH, D_MODEL), 1.0, 0.02)
    w_in = nrm((DEPTH, D_MODEL, N_IN), D_MODEL ** -0.5)
    s5_lambda_re = near((DEPTH, 2, S5_GROUPS, S5_STATE), -0.5, 0.01)
    s5_lambda_im = jnp.pi * jnp.arange(S5_STATE, dtype=jnp.float32) + nrm((DEPTH, 2, S5_GROUPS, S5_STATE), 0.01)
    s5_log_step = unif((DEPTH, 2, S5_GROUPS), math.log(1e-3), math.log(1e-1))
    s5_b_re = nrm((DEPTH, S5_GROUPS, S5_STATE, S5_GROUP), (2 * S5_GROUP) ** -0.5)
    s5_b_im = nrm((DEPTH, S5_GROUPS, S5_STATE, S5_GROUP), (2 * S5_GROUP) ** -0.5)
    s5_c_re = nrm((DEPTH, S5_GROUPS, S5_GROUP, S5_STATE), S5_STATE ** -0.5)
    s5_c_im = nrm((DEPTH, S5_GROUPS, S5_GROUP, S5_STATE), S5_STATE ** -0.5)
    s5_d = nrm((DEPTH, S5_WIDTH), 1.0)
    s5_glu_w = nrm((DEPTH, S5_WIDTH, 2 * S5_WIDTH), S5_WIDTH ** -0.5)
    s5_glu_b = nrm((DEPTH, 2 * S5_WIDTH), 0.01)
    ssd_conv_w = nrm((DEPTH, SSD_CONV, SSD_CONV_CH), SSD_CONV ** -0.5)
    ssd_conv_b = nrm((DEPTH, SSD_CONV_CH), 0.01)
    ssd_a_log = jnp.log(unif((DEPTH, 2, SSD_HEADS), 1.0, 16.0))
    dt0 = jnp.exp(unif((DEPTH, 2, SSD_HEADS), math.log(1e-3), math.log(1e-1)))
    ssd_dt_bias = dt0 + jnp.log(-jnp.expm1(-dt0))
    ssd_d = near((DEPTH, SSD_HEADS), 1.0, 0.1)
    ssd_norm_w = near((DEPTH, SSD_WIDTH), 1.0, 0.02)
    rwkv_mu_rkv = unif((DEPTH, 3, RWKV_WIDTH), 0.0, 1.0)
    rwkv_mu_wag = unif((DEPTH, 3, RWKV_WIDTH), 0.0, 1.0)
    rwkv_w0 = jnp.linspace(-6.0, -1.0, RWKV_WIDTH, dtype=jnp.float32) + nrm((DEPTH, 2, RWKV_WIDTH), 0.1)
    rwkv_w1 = nrm((DEPTH, 2, RWKV_WIDTH, RWKV_DECAY_LORA), RWKV_WIDTH ** -0.5)
    rwkv_w2 = nrm((DEPTH, 2, RWKV_DECAY_LORA, RWKV_WIDTH), 0.1 * RWKV_DECAY_LORA ** -0.5)
    rwkv_a0 = nrm((DEPTH, 2, RWKV_WIDTH), 0.1)
    rwkv_a1 = nrm((DEPTH, 2, RWKV_WIDTH, RWKV_AAA_LORA), RWKV_WIDTH ** -0.5)
    rwkv_a2 = nrm((DEPTH, 2, RWKV_AAA_LORA, RWKV_WIDTH), 0.1 * RWKV_AAA_LORA ** -0.5)
    rwkv_g1 = nrm((DEPTH, RWKV_WIDTH, RWKV_GATE_LORA), RWKV_WIDTH ** -0.5)
    rwkv_g2 = nrm((DEPTH, RWKV_GATE_LORA, RWKV_WIDTH), RWKV_GATE_LORA ** -0.5)
    rwkv_k_k = near((DEPTH, RWKV_WIDTH), 0.85, 0.02)
    rwkv_k_a = near((DEPTH, RWKV_WIDTH), 1.0, 0.02)
    rwkv_r_k = near((DEPTH, RWKV_HEADS, RWKV_HEAD), -0.04, 0.02)
    rwkv_ln_w = near((DEPTH, RWKV_WIDTH), 1.0, 0.02)
    rwkv_ln_b = nrm((DEPTH, RWKV_WIDTH), 0.01)
    proj_a = nrm((DEPTH, S5_WIDTH, D_MODEL), S5_WIDTH ** -0.5)
    proj_b = nrm((DEPTH, SSD_WIDTH, D_MODEL), SSD_WIDTH ** -0.5)
    proj_c = nrm((DEPTH, RWKV_WIDTH, D_MODEL), RWKV_WIDTH ** -0.5)
    w_out = nrm((DEPTH, D_MODEL, D_MODEL), D_MODEL ** -0.5)
    mlp_norm_w = near((DEPTH, D_MODEL), 1.0, 0.02)
    mlp_w1 = nrm((DEPTH, D_MODEL, D_FF), D_MODEL ** -0.5)
    mlp_w2 = nrm((DEPTH, D_FF, D_MODEL), D_FF ** -0.5)
    return {'x': x, 'meta_tokens': meta_tokens, 'final_norm_w': final_norm_w,
            'mix_norm_w': mix_norm_w, 'w_in': w_in,
            's5_lambda_re': s5_lambda_re, 's5_lambda_im': s5_lambda_im, 's5_log_step': s5_log_step,
            's5_b_re': s5_b_re, 's5_b_im': s5_b_im, 's5_c_re': s5_c_re, 's5_c_im': s5_c_im,
            's5_d': s5_d, 's5_glu_w': s5_glu_w, 's5_glu_b': s5_glu_b,
            'ssd_conv_w': ssd_conv_w, 'ssd_conv_b': ssd_conv_b, 'ssd_a_log': ssd_a_log,
            'ssd_dt_bias': ssd_dt_bias, 'ssd_d': ssd_d, 'ssd_norm_w': ssd_norm_w,
            'rwkv_mu_rkv': rwkv_mu_rkv, 'rwkv_mu_wag': rwkv_mu_wag,
            'rwkv_w0': rwkv_w0, 'rwkv_w1': rwkv_w1, 'rwkv_w2': rwkv_w2,
            'rwkv_a0': rwkv_a0, 'rwkv_a1': rwkv_a1, 'rwkv_a2': rwkv_a2,
            'rwkv_g1': rwkv_g1, 'rwkv_g2': rwkv_g2, 'rwkv_k_k': rwkv_k_k, 'rwkv_k_a': rwkv_k_a,
            'rwkv_r_k': rwkv_r_k, 'rwkv_ln_w': rwkv_ln_w, 'rwkv_ln_b': rwkv_ln_b,
            'proj_a': proj_a, 'proj_b': proj_b, 'proj_c': proj_c, 'w_out': w_out,
            'mlp_norm_w': mlp_norm_w, 'mlp_w1': mlp_w1, 'mlp_w2': mlp_w2}


def reference(x, meta_tokens, final_norm_w, mix_norm_w, w_in,
              s5_lambda_re, s5_lambda_im, s5_log_step, s5_b_re, s5_b_im, s5_c_re, s5_c_im,
              s5_d, s5_glu_w, s5_glu_b,
              ssd_conv_w, ssd_conv_b, ssd_a_log, ssd_dt_bias, ssd_d, ssd_norm_w,
              rwkv_mu_rkv, rwkv_mu_wag, rwkv_w0, rwkv_w1, rwkv_w2, rwkv_a0, rwkv_a1, rwkv_a2,
              rwkv_g1, rwkv_g2, rwkv_k_k, rwkv_k_a, rwkv_r_k, rwkv_ln_w, rwkv_ln_b,
              proj_a, proj_b, proj_c, w_out, mlp_norm_w, mlp_w1, mlp_w2):
    bsz = x.shape[0]
    meta = jnp.broadcast_to(meta_tokens[None].astype(x.dtype), (bsz, N_META, D_MODEL))
    h = jnp.concatenate([meta, x], axis=1)
    length = h.shape[1]
    offsets = [int(o) for o in np.cumsum(IN_SPLITS)[:-1]]
    for i in range(DEPTH):
        hn = rms_norm(h, mix_norm_w[i])
        proj = hn @ w_in[i]
        u_a, z_b, xbc_b, dt_b, rkvx_c, gates = jnp.split(proj, offsets, axis=-1)
        r_c, k_c, v_c, x_c = jnp.split(rkvx_c, 4, axis=-1)
        y_a = s5_mixer(u_a, s5_lambda_re[i], s5_lambda_im[i], s5_log_step[i], s5_b_re[i], s5_b_im[i],
                       s5_c_re[i], s5_c_im[i], s5_d[i], s5_glu_w[i], s5_glu_b[i])
        y_b = ssd_mixer(z_b, xbc_b, dt_b, ssd_conv_w[i], ssd_conv_b[i], ssd_a_log[i], ssd_dt_bias[i],
                        ssd_d[i], ssd_norm_w[i])
        y_c = rwkv7_mixer(r_c, k_c, v_c, x_c, rwkv_mu_rkv[i], rwkv_mu_wag[i], rwkv_w0[i], rwkv_w1[i],
                          rwkv_w2[i], rwkv_a0[i], rwkv_a1[i], rwkv_a2[i], rwkv_g1[i], rwkv_g2[i],
                          rwkv_k_k[i], rwkv_k_a[i], rwkv_r_k[i], rwkv_ln_w[i], rwkv_ln_b[i])
        gt = jax.nn.sigmoid(gates.reshape(bsz, length, N_BRANCH, D_MODEL))
        merged = (gt[:, :, 0] * (y_a @ proj_a[i]) + gt[:, :, 1] * (y_b @ proj_b[i])
                  + gt[:, :, 2] * (y_c @ proj_c[i]))
        h = h + merged @ w_out[i]
        hn = rms_norm(h, mlp_norm_w[i])
        h = h + jnp.square(jax.nn.relu(hn @ mlp_w1[i])) @ mlp_w2[i]
    return rms_norm(h, final_norm_w)[:, N_META:]
```

```python
import functools

import jax
import jax.numpy as jnp
from jax import lax
from jax.experimental import pallas as pl
from jax.experimental.pallas import tpu as pltpu

D_MODEL = 1024
N_META = 16
EPS = 1e-6
D_FF = 4 * D_MODEL

S5_WIDTH = 256
S5_GROUP = 16
S5_GROUPS = 16
S5_STATE = 64
S5_Q = 16
S5_PAIRS = S5_GROUPS // 2

SSD_WIDTH = 512
SSD_HEAD_DIM = 64
SSD_HEADS = 8
SSD_GROUPS = 2
SSD_HPG = SSD_HEADS // SSD_GROUPS
SSD_STATE = 128
SSD_CONV = 5
SSD_CHUNK = 128
SSD_CONV_CH = SSD_WIDTH + 2 * SSD_GROUPS * SSD_STATE

RWKV_WIDTH = 256
RWKV_HEAD = 64
RWKV_HEADS = 4
RWKV_LN_EPS = 64e-5
RWKV_CHUNK = 64

N_BRANCH = 3
DT_PAD = 128
ROW_ALIGN = 128
HALO = 8

F32 = jnp.float32
BF16 = jnp.bfloat16
HI = lax.Precision.HIGHEST
VMEM_LIMIT = 56 * 1024 * 1024


def _mm(a, b):
    return jnp.dot(a.astype(BF16), b.astype(BF16), preferred_element_type=F32)


def _mm_hi(a, b):
    return jnp.dot(a, b, preferred_element_type=F32, precision=HI)


def _mm_nt(a, b):
    return lax.dot_general(a.astype(BF16), b.astype(BF16), (((1,), (1,)), ((), ())),
                           preferred_element_type=F32)


def _iota(shape, dim):
    return lax.broadcasted_iota(jnp.int32, shape, dim)


def _params(*sem):
    return pltpu.CompilerParams(dimension_semantics=sem, vmem_limit_bytes=VMEM_LIMIT)


def _const_spec(shape):
    nd = len(shape)
    return pl.BlockSpec(shape, lambda *_: (0,) * nd)


def _in_proj_kernel(h_ref, nw_ref, w_ref, xbc_ref, rkvx_ref, gates_ref, z_ref, ua_ref, dt_ref):
    x = h_ref[...]
    xn = x * lax.rsqrt(jnp.mean(x * x, -1, keepdims=True) + EPS) * nw_ref[...]
    xb = xn.astype(BF16)
    off = 0
    for ref in (xbc_ref, rkvx_ref, gates_ref, z_ref, ua_ref, dt_ref):
        n = ref.shape[-1]
        ref[...] = jnp.dot(xb, w_ref[:, off:off + n], preferred_element_type=F32).astype(ref.dtype)
        off += n


def _in_proj(h2d, norm_w, w_perm, tm):
    t = h2d.shape[0]
    widths = (SSD_CONV_CH, 4 * RWKV_WIDTH, N_BRANCH * D_MODEL, SSD_WIDTH, S5_WIDTH, DT_PAD)
    dtypes = (F32, F32, BF16, BF16, F32, F32)
    n_all = sum(widths)
    return pl.pallas_call(
        _in_proj_kernel,
        grid=(t // tm,),
        in_specs=[pl.BlockSpec((tm, D_MODEL), lambda i: (i, 0)),
                  _const_spec((1, D_MODEL)),
                  pl.BlockSpec((D_MODEL, n_all), lambda i: (0, 0), pipeline_mode=pl.Buffered(1))],
        out_specs=[pl.BlockSpec((tm, n), lambda i: (i, 0)) for n in widths],
        out_shape=[jax.ShapeDtypeStruct((t, n), dt) for n, dt in zip(widths, dtypes)],
        compiler_params=_params("parallel"),
        name="in_proj",
    )(h2d, norm_w.reshape(1, D_MODEL), w_perm)


def _permute_w_in(w):
    o_u, o_z, o_xbc = 0, S5_WIDTH, S5_WIDTH + SSD_WIDTH
    o_dt = o_xbc + SSD_CONV_CH
    o_rkvx = o_dt + SSD_HEADS
    o_g = o_rkvx + 4 * RWKV_WIDTH
    dt_cols = jnp.pad(w[:, o_dt:o_rkvx], ((0, 0), (0, DT_PAD - SSD_HEADS)))
    return jnp.concatenate([w[:, o_xbc:o_dt], w[:, o_rkvx:o_g], w[:, o_g:], w[:, o_z:o_xbc],
                            w[:, o_u:o_z], dt_cols], axis=1).astype(BF16)


def _s5_tables(lam_re, lam_im, log_step, b_re, b_im, c_re, c_im):
    q, g, n, hh = S5_Q, S5_GROUPS, S5_STATE, S5_GROUP
    step = jnp.exp(log_step)[:, :, None]
    j = jnp.arange(q + 1, dtype=F32)[:, None, None, None]
    mag = jnp.exp(lam_re * step * j)
    ang = lam_im * step * j
    pw_re, pw_im = mag * jnp.cos(ang), mag * jnp.sin(ang)
    ab_re, ab_im = pw_re[1], pw_im[1]
    den = lam_re * lam_re + lam_im * lam_im
    co_re = ((ab_re - 1.0) * lam_re + ab_im * lam_im) / den
    co_im = (ab_im * lam_re - (ab_re - 1.0) * lam_im) / den
    eb_re = co_re[..., None] * b_re - co_im[..., None] * b_im
    eb_im = co_re[..., None] * b_im + co_im[..., None] * b_re
    cp_re = c_re[None, None] * pw_re[:, :, :, None, :] - c_im[None, None] * pw_im[:, :, :, None, :]
    cp_im = c_re[None, None] * pw_im[:, :, :, None, :] + c_im[None, None] * pw_re[:, :, :, None, :]
    kern = (jnp.einsum('jdgon,dgni->jdgoi', cp_re, eb_re, precision=HI)
            - jnp.einsum('jdgon,dgni->jdgoi', cp_im, eb_im, precision=HI))
    kf, kb = kern[:q, 0], kern[:q, 1]
    lag = jnp.arange(q)[None, :] - jnp.arange(q)[:, None]
    alag = jnp.abs(lag)
    sel = lag[:, :, None, None, None]
    tst = jnp.where(sel > 0, kf[alag], jnp.where(sel < 0, kb[alag], (kf[0] + kb[0])[None, None]))
    toe = jnp.transpose(tst, (2, 0, 4, 1, 3)).reshape(g, q * hh, q * hh)

    def cmul(ar, ai, br, bi):
        return ar * br - ai * bi, ar * bi + ai * br

    pf_re, pf_im = pw_re[q - 1::-1, 0][:q], pw_im[q - 1::-1, 0][:q]
    pb_re, pb_im = pw_re[:q, 1], pw_im[:q, 1]
    sf_re, sf_im = cmul(pf_re[..., None], pf_im[..., None], eb_re[0][None], eb_im[0][None])
    sb_re, sb_im = cmul(pb_re[..., None], pb_im[..., None], eb_re[1][None], eb_im[1][None])
    so = jnp.stack([sf_re, sf_im, sb_re, sb_im], 0)
    so = jnp.transpose(so, (2, 1, 4, 0, 3)).reshape(g, q * hh, 4, n)
    rf_re, rf_im = cp_re[1:q + 1, 0], -cp_im[1:q + 1, 0]
    rb_re, rb_im = cp_re[q:0:-1, 1], -cp_im[q:0:-1, 1]
    ri = jnp.stack([rf_re, rf_im, rb_re, rb_im], 0)
    ri = jnp.transpose(ri, (2, 0, 4, 1, 3)).reshape(g, 4, n, q * hh)

    eye2 = jnp.eye(2, dtype=F32)
    npair = S5_PAIRS
    t2 = jnp.einsum('pgrc,gh->pgrhc', toe.reshape(npair, 2, q * hh, q * hh), eye2)
    t2 = t2.reshape(npair, 2 * q * hh, 2 * q * hh)
    so2 = jnp.einsum('pgrkn,gh->pgrkhn', so.reshape(npair, 2, q * hh, 4, n), eye2)
    so2 = so2.reshape(npair, 2 * q * hh, 4 * 2 * n)
    r2 = jnp.einsum('pgknr,gh->pkgnhr', ri.reshape(npair, 2, 4, n, q * hh), eye2)
    r2 = r2.reshape(npair, 4 * 2 * n, 2 * q * hh)
    apow = jnp.stack([pw_re[q, 0], pw_im[q, 0], pw_re[q, 1], pw_im[q, 1]], 0)
    apow = jnp.transpose(apow.reshape(4, npair, 2, n), (1, 0, 2, 3)).reshape(1, npair * 4 * 2 * n)
    return t2, so2, r2, apow


def _s5_send_kernel(z_ref, so_ref, send_ref):
    send_ref[0] = _mm_hi(z_ref[0], so_ref[0])


def _s5_scan_kernel(send_ref, apow_ref, hin_ref):
    nc = send_ref.shape[1]
    lanes = 2 * S5_STATE
    apow = apow_ref[...]

    def piece(row, p, k):
        o = (p * 4 + k) * lanes
        return row[:, o:o + lanes]

    sub = _iota((HALO, lanes), 0)

    def body(i, carry):
        starts = (pl.multiple_of(i * HALO, HALO), pl.multiple_of(nc - HALO - i * HALO, HALO))
        tiles = tuple(send_ref[0, pl.ds(s, HALO), :] for s in starts)
        new = []
        for p in range(S5_PAIRS):
            for d in range(2):
                h_re, h_im = carry[(p * 2 + d) * 2], carry[(p * 2 + d) * 2 + 1]
                a_re, a_im = piece(apow, p, 2 * d), piece(apow, p, 2 * d + 1)
                s_re, s_im = piece(tiles[d], p, 2 * d), piece(tiles[d], p, 2 * d + 1)
                out_re = jnp.zeros((HALO, lanes), F32)
                out_im = jnp.zeros((HALO, lanes), F32)
                for step in range(HALO):
                    rix = HALO - 1 - step if d else step
                    out_re = jnp.where(sub == rix, h_re, out_re)
                    out_im = jnp.where(sub == rix, h_im, out_im)
                    n_re = a_re * h_re - a_im * h_im + s_re[rix:rix + 1, :]
                    n_im = a_re * h_im + a_im * h_re + s_im[rix:rix + 1, :]
                    h_re, h_im = n_re, n_im
                o_re = (p * 4 + 2 * d) * lanes
                hin_ref[0, pl.ds(starts[d], HALO), o_re:o_re + lanes] = out_re
                hin_ref[0, pl.ds(starts[d], HALO), o_re + lanes:o_re + 2 * lanes] = out_im
                new += [h_re, h_im]
        return tuple(new)

    zero = jnp.zeros((1, lanes), F32)
    lax.fori_loop(0, nc // HALO, body, (zero,) * (S5_PAIRS * 4))


def _s5_out_kernel(z_ref, hin_ref, t2_ref, r2_ref, y_ref):
    y_ref[0] = _mm_hi(z_ref[0], t2_ref[0]) + _mm_hi(hin_ref[0], r2_ref[0])


def _s5_mixer_core(u, tables):
    t2, so2, r2, apow = tables
    bsz, lp, _ = u.shape
    nc = lp // S5_Q
    width = S5_GROUPS * S5_Q * S5_GROUP
    pw = width // S5_PAIRS
    z = jnp.transpose(u.reshape(bsz, nc, S5_Q, S5_GROUPS, S5_GROUP), (0, 1, 3, 2, 4)).reshape(bsz, nc, width)
    blk = pl.BlockSpec((1, nc, pw), lambda p, b: (b, 0, p))
    wblk = pl.BlockSpec((1, pw, pw), lambda p, b: (p, 0, 0))
    send = pl.pallas_call(
        _s5_send_kernel, grid=(S5_PAIRS, bsz), in_specs=[blk, wblk], out_specs=blk,
        out_shape=jax.ShapeDtypeStruct((bsz, nc, width), F32),
        compiler_params=_params("parallel", "parallel"), name="s5_send")(z, so2)
    row = pl.BlockSpec((1, nc, width), lambda b: (b, 0, 0))
    hin = pl.pallas_call(
        _s5_scan_kernel, grid=(bsz,), in_specs=[row, _const_spec((1, width))], out_specs=row,
        out_shape=jax.ShapeDtypeStruct((bsz, nc, width), F32),
        compiler_params=_params("parallel"), name="s5_scan")(send, apow)
    y = pl.pallas_call(
        _s5_out_kernel, grid=(S5_PAIRS, bsz), in_specs=[blk, blk, wblk, wblk], out_specs=blk,
        out_shape=jax.ShapeDtypeStruct((bsz, nc, width), F32),
        compiler_params=_params("parallel", "parallel"), name="s5_out")(z, hin, t2, r2)
    y = jnp.transpose(y.reshape(bsz, nc, S5_GROUPS, S5_Q, S5_GROUP), (0, 1, 3, 2, 4))
    return y.reshape(bsz, lp, S5_WIDTH)


def _ssd_kernel(cur_ref, prev_ref, next_ref, dt_ref, cw_ref, cb_ref, dtb_ref, aneg_ref, dskip_ref,
                y_ref, ext_ref, h_ref, *, reverse, front, add_skip):
    c = pl.program_id(1)
    nc = pl.num_programs(1)
    cm = nc - 1 - c if reverse else c
    q = SSD_CHUNK

    @pl.when(c == 0)
    def _():
        h_ref[...] = jnp.zeros_like(h_ref)

    has_prev = jnp.where(cm > 0, 1.0, 0.0)
    has_next = jnp.where(cm < nc - 1, 1.0, 0.0)
    ext_ref[0:HALO, :] = prev_ref[0] * has_prev
    ext_ref[HALO:HALO + q, :] = cur_ref[0]
    ext_ref[HALO + q:HALO + q + HALO, :] = next_ref[0] * has_next
    acc = jnp.zeros((q, SSD_CONV_CH), F32) + cb_ref[...]
    pad = SSD_CONV // 2
    for k in range(SSD_CONV):
        acc = acc + cw_ref[k:k + 1, :] * ext_ref[HALO - pad + k:HALO - pad + k + q, :]
    valid = (cm * q + _iota((q, 1), 0)) >= front
    xc = jnp.where(valid, acc * jax.nn.sigmoid(acc), 0.0)
    x = xc[:, :SSD_WIDTH]
    bm = xc[:, SSD_WIDTH:SSD_WIDTH + SSD_GROUPS * SSD_STATE]
    cmat = xc[:, SSD_WIDTH + SSD_GROUPS * SSD_STATE:]

    dt = jnp.where(valid, jax.nn.softplus(dt_ref[0] + dtb_ref[...]), 0.0)
    a = dt * aneg_ref[...]
    rr, cc = _iota((q, q), 0), _iota((q, q), 1)
    tri = (cc >= rr) if reverse else (cc <= rr)
    cs = _mm_hi(tri.astype(F32), a)
    cs_t = cs.T
    tot = cs[0:1, :] if reverse else cs[q - 1:q, :]
    e_cs = jnp.exp(cs)
    to_end = jnp.exp(tot - cs)
    chunk_decay = jnp.exp(tot)

    for g in range(SSD_GROUPS):
        bg = bm[:, g * SSD_STATE:(g + 1) * SSD_STATE]
        cg = cmat[:, g * SSD_STATE:(g + 1) * SSD_STATE]
        cb = _mm_nt(cg, bg)
        for jj in range(SSD_HPG):
            j = g * SSD_HPG + jj
            xh = x[:, j * SSD_HEAD_DIM:(j + 1) * SSD_HEAD_DIM]
            xdt = xh * dt[:, j:j + 1]
            seg = jnp.exp(jnp.where(tri, cs[:, j:j + 1] - cs_t[j:j + 1, :], -jnp.inf))
            y = _mm(cb * seg, xdt)
            h_in = h_ref[j]
            y = y + _mm(cg, h_in) * e_cs[:, j:j + 1]
            st = _mm((bg * to_end[:, j:j + 1]).T, xdt)
            h_ref[j] = h_in * chunk_decay[:, j:j + 1] + st
            if add_skip:
                y = y + dskip_ref[:, j * SSD_HEAD_DIM:(j + 1) * SSD_HEAD_DIM] * xh
            y_ref[0, :, j * SSD_HEAD_DIM:(j + 1) * SSD_HEAD_DIM] = y


def _ssd_direction(xbc, dt_raw, conv_w, conv_b, dt_bias, a_log, d_skip, *, reverse, front, add_skip):
    bsz, lp, _ = xbc.shape
    nc = lp // SSD_CHUNK
    per = SSD_CHUNK // HALO
    nh = lp // HALO

    def cm(c):
        return nc - 1 - c if reverse else c

    cur = pl.BlockSpec((1, SSD_CHUNK, SSD_CONV_CH), lambda b, c: (b, cm(c), 0))
    prv = pl.BlockSpec((1, HALO, SSD_CONV_CH), lambda b, c: (b, jnp.maximum(cm(c) * per - 1, 0), 0))
    nxt = pl.BlockSpec((1, HALO, SSD_CONV_CH), lambda b, c: (b, jnp.minimum(cm(c) * per + per, nh - 1), 0))
    dts = pl.BlockSpec((1, SSD_CHUNK, DT_PAD), lambda b, c: (b, cm(c), 0))
    cw = jnp.pad(conv_w, ((0, HALO - SSD_CONV), (0, 0)))
    dtb = jnp.pad(dt_bias, (0, DT_PAD - SSD_HEADS)).reshape(1, DT_PAD)
    aneg = jnp.pad(-jnp.exp(a_log), (0, DT_PAD - SSD_HEADS)).reshape(1, DT_PAD)
    dsk = jnp.repeat(d_skip, SSD_HEAD_DIM).reshape(1, SSD_WIDTH)
    kern = functools.partial(_ssd_kernel, reverse=reverse, front=front, add_skip=add_skip)
    return pl.pallas_call(
        kern, grid=(bsz, nc),
        in_specs=[cur, prv, nxt, dts, _const_spec((HALO, SSD_CONV_CH)), _const_spec((1, SSD_CONV_CH)),
                  _const_spec((1, DT_PAD)), _const_spec((1, DT_PAD)), _const_spec((1, SSD_WIDTH))],
        out_specs=pl.BlockSpec((1, SSD_CHUNK, SSD_WIDTH), lambda b, c: (b, cm(c), 0)),
        out_shape=jax.ShapeDtypeStruct((bsz, lp, SSD_WIDTH), F32),
        scratch_shapes=[pltpu.VMEM((SSD_CHUNK + 2 * HALO, SSD_CONV_CH), F32),
                        pltpu.VMEM((SSD_HEADS, SSD_STATE, SSD_HEAD_DIM), F32)],
        compiler_params=_params("parallel", "arbitrary"),
        name="ssd_bwd" if reverse else "ssd_fwd",
    )(xbc, xbc, xbc, dt_raw, cw, conv_b.reshape(1, SSD_CONV_CH), dtb, aneg, dsk)


def _head_ones():
    rr, cc = _iota((RWKV_WIDTH, RWKV_WIDTH), 0), _iota((RWKV_WIDTH, RWKV_WIDTH), 1)
    return (jnp.right_shift(rr, 6) == jnp.right_shift(cc, 6)).astype(F32)


def _rwkv_pre_kernel(cur_ref, prev_ref, next_ref, mu_ref, vec_ref, g1_ref, g2_ref, w1_ref, w2_ref, a1_ref,
                     a2_ref, w0a0_ref,
                     r_ref, v_ref, nkk_ref, g_ref, bonus_ref, lw0_ref, kd0_ref, b0_ref, lw1_ref, kd1_ref,
                     b1_ref, *, front):
    j = pl.program_id(1)
    nb = pl.num_programs(1)
    tb = cur_ref.shape[1]
    w = RWKV_WIDTH
    cur = cur_ref[0]
    prow = prev_ref[0, HALO - 1:HALO, :] * jnp.where(j > 0, 1.0, 0.0)
    nrow = next_ref[0, 0:1, :] * jnp.where(j < nb - 1, 1.0, 0.0)
    rowi = _iota((tb, 1), 0)
    prev = jnp.where(rowi == 0, prow, pltpu.roll(cur, 1, 0))
    nxt = jnp.where(rowi == tb - 1, nrow, pltpu.roll(cur, tb - 1, 0))
    shift = 0.5 * (prev + nxt) - cur
    valid = (j * tb + rowi) >= front
    r = cur[:, 0:w] + shift[:, 0:w] * mu_ref[0:1, :]
    k = jnp.where(valid, cur[:, w:2 * w] + shift[:, w:2 * w] * mu_ref[1:2, :], 0.0)
    v = jnp.where(valid, cur[:, 2 * w:3 * w] + shift[:, 2 * w:3 * w] * mu_ref[2:3, :], 0.0)
    xc, dxc = cur[:, 3 * w:], shift[:, 3 * w:]
    xw = xc + dxc * mu_ref[3:4, :]
    xa = xc + dxc * mu_ref[4:5, :]
    xg = xc + dxc * mu_ref[5:6, :]
    k_k, k_a, r_k = vec_ref[0:1, :], vec_ref[1:2, :], vec_ref[2:3, :]
    ones = _head_ones()
    g_ref[0] = _mm(jax.nn.sigmoid(_mm(xg, g1_ref[...])), g2_ref[...])
    kk = k * k_k
    kk = kk * lax.rsqrt(_mm_hi(kk * kk, ones) + 1e-12)
    r_ref[0] = r
    v_ref[0] = v
    nkk_ref[0] = -kk
    bonus_ref[0] = _mm_hi(r * k * r_k, ones) * v
    for d, (lw_ref, kd_ref, b_ref) in enumerate(((lw0_ref, kd0_ref, b0_ref), (lw1_ref, kd1_ref, b1_ref))):
        lw = w0a0_ref[d:d + 1, :] + _mm(jnp.tanh(_mm(xw, w1_ref[d])), w2_ref[d])
        w_log = -jax.nn.softplus(-lw) - 0.5
        lw_ref[0] = -jnp.exp(w_log)
        ag = jax.nn.sigmoid(w0a0_ref[2 + d:3 + d, :] + _mm(_mm(xa, a1_ref[d]), a2_ref[d]))
        kd_ref[0] = k * (1.0 + (ag - 1.0) * k_a)
        b_ref[0] = kk * ag


def _rwkv_pre(rkvx, mu, vec, g1, g2, w1, w2, a1, a2, w0a0, *, front, tb):
    bsz, lp, _ = rkvx.shape
    per = tb // HALO
    nh = lp // HALO
    w = RWKV_WIDTH
    cur = pl.BlockSpec((1, tb, 4 * w), lambda b, j: (b, j, 0))
    prv = pl.BlockSpec((1, HALO, 4 * w), lambda b, j: (b, jnp.maximum(j * per - 1, 0), 0))
    nxt = pl.BlockSpec((1, HALO, 4 * w), lambda b, j: (b, jnp.minimum(j * per + per, nh - 1), 0))
    out = pl.BlockSpec((1, tb, w), lambda b, j: (b, j, 0))
    consts = [mu, vec, g1, g2, w1, w2, a1, a2, w0a0]
    return pl.pallas_call(
        functools.partial(_rwkv_pre_kernel, front=front), grid=(bsz, lp // tb),
        in_specs=[cur, prv, nxt] + [_const_spec(c.shape) for c in consts],
        out_specs=[out] * 11,
        out_shape=[jax.ShapeDtypeStruct((bsz, lp, w), F32)] * 11,
        compiler_params=_params("parallel", "parallel"),
        name="rwkv_pre",
    )(rkvx, rkvx, rkvx, *consts)


def _rwkv_scan_kernel(r_ref, v_ref, nkk_ref, lw_ref, kd_ref, b_ref, y_ref, st_ref, *, reverse):
    c = pl.program_id(1)
    n = RWKV_CHUNK
    w = RWKV_WIDTH

    @pl.when(c == 0)
    def _():
        st_ref[...] = jnp.zeros_like(st_ref)

    lane_head = jnp.right_shift(_iota((n, w), 1), 6)

    def stack(x):
        return jnp.concatenate([jnp.where(lane_head == h, x, 0.0) for h in range(RWKV_HEADS)], axis=0)

    def unstack(x):
        out = x[0:n]
        for h in range(1, RWKV_HEADS):
            out = out + x[h * n:(h + 1) * n]
        return out

    rr, cc = _iota((w, w), 0), _iota((w, w), 1)
    same = jnp.right_shift(rr, 6) == jnp.right_shift(cc, 6)
    tl, sl = jnp.bitwise_and(rr, n - 1), jnp.bitwise_and(cc, n - 1)
    strict = same & ((sl > tl) if reverse else (sl < tl))
    incl = same & ((sl >= tl) if reverse else (sl <= tl))
    eye = (rr == cc).astype(F32)

    lw = lw_ref[0]
    r, v, a, kd, b = r_ref[0], v_ref[0], nkk_ref[0], kd_ref[0], b_ref[0]
    r8, c8 = _iota((n, n), 0), _iota((n, n), 1)
    tri = ((c8 >= r8) if reverse else (c8 <= r8)).astype(F32)
    cum = _mm_hi(tri, lw)
    tot = cum[0:1, :] if reverse else cum[n - 1:n, :]
    p_inv = jnp.exp(-cum)
    p_end = jnp.exp(tot - cum)
    r_t = stack(r * jnp.exp(cum))
    a_t = stack(a * jnp.exp(cum - lw))
    b_t = stack(b * p_inv)
    k_t = stack(kd * p_inv)
    b_e = stack(b * p_end)
    k_e = stack(kd * p_end)
    vs = stack(v)

    lhs = jnp.concatenate([a_t, r_t], axis=0)
    rhs = jnp.concatenate([b_t, k_t], axis=0)
    gram = lax.dot_general(lhs, rhs, (((1,), (1,)), ((), ())), preferred_element_type=F32, precision=HI)
    a_ab = jnp.where(strict, gram[0:w, 0:w], 0.0)
    a_ak = jnp.where(strict, gram[0:w, w:2 * w], 0.0)
    a_rb = jnp.where(incl, gram[w:2 * w, 0:w], 0.0)
    a_rk = jnp.where(incl, gram[w:2 * w, w:2 * w], 0.0)

    inv = eye + a_ab
    pw = a_ab
    for _ in range(RWKV_CHUNK.bit_length() - 2):
        pw = _mm_hi(pw, pw)
        inv = inv + _mm_hi(inv, pw)

    wmat = _mm_hi(inv, a_t)
    u0 = _mm_hi(inv, _mm_hi(a_ak, vs))
    y0 = _mm_hi(a_rb, u0) + _mm_hi(a_rk, vs)
    rp = r_t + _mm_hi(a_rb, wmat)
    st = st_ref[...]
    y_ref[0] = unstack(y0 + _mm_hi(rp, st))
    decay = jnp.exp(tot)
    mk = eye * decay + _mm_hi(b_e.T, wmat)
    nk = _mm_hi(b_e.T, u0) + _mm_hi(k_e.T, vs)
    st_ref[...] = _mm_hi(mk, st) + nk


def _rwkv_scan(r, v, nkk, lw, kd, b, *, reverse):
    bsz, lp, w = r.shape
    nc = lp // RWKV_CHUNK

    def cm(c):
        return nc - 1 - c if reverse else c

    blk = pl.BlockSpec((1, RWKV_CHUNK, w), lambda bb, c: (bb, cm(c), 0))
    return pl.pallas_call(
        functools.partial(_rwkv_scan_kernel, reverse=reverse), grid=(bsz, nc),
        in_specs=[blk] * 6, out_specs=blk,
        out_shape=jax.ShapeDtypeStruct((bsz, lp, w), F32),
        scratch_shapes=[pltpu.VMEM((w, w), F32)],
        compiler_params=_params("parallel", "arbitrary"),
        name="rwkv_bwd" if reverse else "rwkv_fwd",
    )(r, v, nkk, lw, kd, b)


def _merge_kernel(h_ref, gates_ref, ys5_ref, ua_ref, ysf_ref, ysb_ref, z_ref, ycf_ref, ycb_ref, g_ref,
                  bonus_ref, vec_a_ref, glu_w_ref, glu_b_ref, ssd_nw_ref, ln_ref, pa_ref, pb_ref, pc_ref,
                  wo_ref, o_ref):
    ua = ua_ref[...]
    ya = jax.nn.gelu(ys5_ref[...] + vec_a_ref[...] * ua)
    zz = _mm(ya, glu_w_ref[...]) + glu_b_ref[...]
    ya = zz[:, :S5_WIDTH] * jax.nn.sigmoid(zz[:, S5_WIDTH:])
    z = z_ref[...].astype(F32)
    yb = (ysf_ref[...] + ysb_ref[...]) * (z * jax.nn.sigmoid(z))
    yb = yb * lax.rsqrt(jnp.mean(yb * yb, -1, keepdims=True) + EPS) * ssd_nw_ref[...]
    yc = ycf_ref[...] + ycb_ref[...]
    avg = _head_ones() * (1.0 / RWKV_HEAD)
    mean = _mm_hi(yc, avg)
    dev = yc - mean
    var = _mm_hi(dev * dev, avg)
    yc = dev * lax.rsqrt(var + RWKV_LN_EPS) * ln_ref[0:1, :] + ln_ref[1:2, :]
    yc = (yc + bonus_ref[...]) * g_ref[...]
    d = D_MODEL
    merged = (jax.nn.sigmoid(gates_ref[:, 0:d].astype(F32)) * _mm(ya, pa_ref[...])
              + jax.nn.sigmoid(gates_ref[:, d:2 * d].astype(F32)) * _mm(yb, pb_ref[...])
              + jax.nn.sigmoid(gates_ref[:, 2 * d:3 * d].astype(F32)) * _mm(yc, pc_ref[...]))
    o_ref[...] = h_ref[...] + _mm(merged, wo_ref[...])


def _merge(h2d, gates, ys5, ua, ysf, ysb, z, ycf, ycb, g, bonus, consts, tm):
    t = h2d.shape[0]
    toks = [h2d, gates, ys5, ua, ysf, ysb, z, ycf, ycb, g, bonus]
    return pl.pallas_call(
        _merge_kernel, grid=(t // tm,),
        in_specs=[pl.BlockSpec((tm, a.shape[1]), lambda i: (i, 0)) for a in toks]
        + [_const_spec(c.shape) for c in consts],
        out_specs=pl.BlockSpec((tm, D_MODEL), lambda i: (i, 0)),
        out_shape=jax.ShapeDtypeStruct((t, D_MODEL), F32),
        compiler_params=_params("parallel"),
        name="merge",
    )(*toks, *consts)


def _mlp_kernel(h_ref, nw_ref, w1_ref, w2_ref, fw_ref, o_ref, *, front, rows_per_batch, final):
    x = h_ref[...]
    tm = x.shape[0]
    xn = x * lax.rsqrt(jnp.mean(x * x, -1, keepdims=True) + EPS) * nw_ref[...]
    a = jnp.maximum(jnp.dot(xn.astype(BF16), w1_ref[...], preferred_element_type=F32), 0.0)
    y = x + jnp.dot((a * a).astype(BF16), w2_ref[...], preferred_element_type=F32)
    if final:
        y = y * lax.rsqrt(jnp.mean(y * y, -1, keepdims=True) + EPS) * fw_ref[...]
    row = (pl.program_id(0) * tm) % rows_per_batch + _iota((tm, 1), 0)
    row = jnp.where(row >= rows_per_batch, row - rows_per_batch, row)
    o_ref[...] = jnp.where(row >= front, y, 0.0)


def _mlp(h2d, norm_w, w1, w2, final_w, *, front, rows_per_batch, final, tm):
    t = h2d.shape[0]
    kern = functools.partial(_mlp_kernel, front=front, rows_per_batch=rows_per_batch, final=final)
    return pl.pallas_call(
        kern, grid=(t // tm,),
        in_specs=[pl.BlockSpec((tm, D_MODEL), lambda i: (i, 0)), _const_spec((1, D_MODEL)),
                  pl.BlockSpec((D_MODEL, D_FF), lambda i: (0, 0), pipeline_mode=pl.Buffered(1)),
                  pl.BlockSpec((D_FF, D_MODEL), lambda i: (0, 0), pipeline_mode=pl.Buffered(1)),
                  _const_spec((1, D_MODEL))],
        out_specs=pl.BlockSpec((tm, D_MODEL), lambda i: (i, 0)),
        out_shape=jax.ShapeDtypeStruct((t, D_MODEL), F32),
        compiler_params=_params("parallel"),
        name="mlp",
    )(h2d, norm_w.reshape(1, D_MODEL), w1.astype(BF16), w2.astype(BF16), final_w.reshape(1, D_MODEL))


def _token_tile(t, cap):
    tm = cap
    while t % tm:
        tm //= 2
    return tm


def _row_tile(lp, cap):
    k = lp // ROW_ALIGN
    best = 1
    for m in range(1, k + 1):
        if k % m == 0 and m * ROW_ALIGN <= cap:
            best = m
    return best * ROW_ALIGN


def kernel(x, meta_tokens, final_norm_w, mix_norm_w, w_in, s5_lambda_re, s5_lambda_im, s5_log_step, s5_b_re, s5_b_im, s5_c_re, s5_c_im, s5_d, s5_glu_w, s5_glu_b, ssd_conv_w, ssd_conv_b, ssd_a_log, ssd_dt_bias, ssd_d, ssd_norm_w, rwkv_mu_rkv, rwkv_mu_wag, rwkv_w0, rwkv_w1, rwkv_w2, rwkv_a0, rwkv_a1, rwkv_a2, rwkv_g1, rwkv_g2, rwkv_k_k, rwkv_k_a, rwkv_r_k, rwkv_ln_w, rwkv_ln_b, proj_a, proj_b, proj_c, w_out, mlp_norm_w, mlp_w1, mlp_w2):
    bsz, seq, d = x.shape
    assert d == D_MODEL
    length = N_META + seq
    lp = -(-length // ROW_ALIGN) * ROW_ALIGN
    front = lp - length
    t = bsz * lp
    depth = w_in.shape[0]
    meta = jnp.broadcast_to(meta_tokens[None].astype(x.dtype), (bsz, N_META, d))
    h = jnp.concatenate([jnp.zeros((bsz, front, d), x.dtype), meta, x], axis=1).reshape(t, d)
    tm = _token_tile(t, 512)
    tb = _row_tile(lp, 512)

    for i in range(depth):
        xbc, rkvx, gates, z, ua, dt_raw = _in_proj(h, mix_norm_w[i], _permute_w_in(w_in[i]), tm)
        tables = _s5_tables(s5_lambda_re[i], s5_lambda_im[i], s5_log_step[i], s5_b_re[i], s5_b_im[i],
                            s5_c_re[i], s5_c_im[i])
        ys5 = _s5_mixer_core(ua.reshape(bsz, lp, S5_WIDTH), tables).reshape(t, S5_WIDTH)
        xbc3 = xbc.reshape(bsz, lp, SSD_CONV_CH)
        dt3 = dt_raw.reshape(bsz, lp, DT_PAD)
        ys = [_ssd_direction(xbc3, dt3, ssd_conv_w[i], ssd_conv_b[i], ssd_dt_bias[i, dr], ssd_a_log[i, dr],
                             ssd_d[i], reverse=bool(dr), front=front, add_skip=(dr == 0)).reshape(t, SSD_WIDTH)
              for dr in range(2)]
        mu = jnp.concatenate([rwkv_mu_rkv[i], rwkv_mu_wag[i]], axis=0)
        vec = jnp.stack([rwkv_k_k[i], rwkv_k_a[i], rwkv_r_k[i].reshape(RWKV_WIDTH)], axis=0)
        w0a0 = jnp.concatenate([rwkv_w0[i], rwkv_a0[i]], axis=0)
        pre = _rwkv_pre(rkvx.reshape(bsz, lp, 4 * RWKV_WIDTH), mu, vec, rwkv_g1[i], rwkv_g2[i], rwkv_w1[i],
                        rwkv_w2[i], rwkv_a1[i], rwkv_a2[i], w0a0, front=front, tb=tb)
        r, v, nkk, g, bonus, lw0, kd0, b0, lw1, kd1, b1 = pre
        ycf = _rwkv_scan(r, v, nkk, lw0, kd0, b0, reverse=False).reshape(t, RWKV_WIDTH)
        ycb = _rwkv_scan(r, v, nkk, lw1, kd1, b1, reverse=True).reshape(t, RWKV_WIDTH)
        consts = [s5_d[i].reshape(1, S5_WIDTH), s5_glu_w[i].astype(BF16), s5_glu_b[i].reshape(1, 2 * S5_WIDTH),
                  ssd_norm_w[i].reshape(1, SSD_WIDTH), jnp.stack([rwkv_ln_w[i], rwkv_ln_b[i]], axis=0),
                  proj_a[i].astype(BF16), proj_b[i].astype(BF16), proj_c[i].astype(BF16),
                  w_out[i].astype(BF16)]
        h = _merge(h, gates, ys5, ua, ys[0], ys[1], z, ycf, ycb, g.reshape(t, RWKV_WIDTH),
                   bonus.reshape(t, RWKV_WIDTH), consts, tm)
        h = _mlp(h, mlp_norm_w[i], mlp_w1[i], mlp_w2[i], final_norm_w, front=front, rows_per_batch=lp,
                 final=(i == depth - 1), tm=tm)
    return h.reshape(bsz, lp, d)[:, front + N_META:]
```

```python
import functools

import jax
import jax.numpy as jnp
from jax import lax
from jax.experimental import pallas as pl
from jax.experimental.pallas import tpu as pltpu

D_MODEL = 1024
N_META = 16
EPS = 1e-6
D_FF = 4 * D_MODEL

S5_WIDTH = 256
S5_GROUP = 16
S5_GROUPS = 16
S5_STATE = 64
S5_Q = 16
S5_PAIRS = S5_GROUPS // 2

SSD_WIDTH = 512
SSD_HEAD_DIM = 64
SSD_HEADS = 8
SSD_GROUPS = 2
SSD_HPG = SSD_HEADS // SSD_GROUPS
SSD_STATE = 128
SSD_CONV = 5
SSD_CHUNK = 128
SSD_CONV_CH = SSD_WIDTH + 2 * SSD_GROUPS * SSD_STATE

RWKV_WIDTH = 256
RWKV_HEAD = 64
RWKV_HEADS = 4
RWKV_LN_EPS = 64e-5
RWKV_CHUNK = 64

N_BRANCH = 3
DT_PAD = 128
ROW_ALIGN = 128
HALO = 8

F32 = jnp.float32
BF16 = jnp.bfloat16
HI = lax.Precision.HIGHEST
VMEM_LIMIT = 56 * 1024 * 1024


def _mm(a, b):
    return jnp.dot(a.astype(BF16), b.astype(BF16), preferred_element_type=F32)


def _mm_hi(a, b):
    return jnp.dot(a, b, preferred_element_type=F32, precision=HI)


def _mm_split(a, b):
    hi = a.astype(BF16)
    lo = (a - hi.astype(F32)).astype(BF16)
    bb = b.astype(BF16)
    return jnp.dot(hi, bb, preferred_element_type=F32) + jnp.dot(lo, bb, preferred_element_type=F32)


def _mm_x3(a, b):
    ah = a.astype(BF16)
    al = (a - ah.astype(F32)).astype(BF16)
    bh = b.astype(BF16)
    bl = (b - bh.astype(F32)).astype(BF16)
    return (jnp.dot(ah, bh, preferred_element_type=F32) + jnp.dot(al, bh, preferred_element_type=F32)
            + jnp.dot(ah, bl, preferred_element_type=F32))


def _mm_nt(a, b):
    return lax.dot_general(a.astype(BF16), b.astype(BF16), (((1,), (1,)), ((), ())),
                           preferred_element_type=F32)


def _mm_nt_hi(a, b):
    return lax.dot_general(a, b, (((1,), (1,)), ((), ())), preferred_element_type=F32, precision=HI)


def _iota(shape, dim):
    return lax.broadcasted_iota(jnp.int32, shape, dim)


def _params(*sem):
    return pltpu.CompilerParams(dimension_semantics=sem, vmem_limit_bytes=VMEM_LIMIT)


def _const_spec(shape):
    nd = len(shape)
    return pl.BlockSpec(shape, lambda *_: (0,) * nd)


def _in_proj_kernel(h_ref, nw_ref, w_ref, xbc_ref, rkvx_ref, gates_ref, z_ref, ua_ref, dt_ref):
    x = h_ref[...]
    xn = x * lax.rsqrt(jnp.mean(x * x, -1, keepdims=True) + EPS) * nw_ref[...]
    xb = xn.astype(BF16)
    off = 0
    for ref in (xbc_ref, rkvx_ref, gates_ref, z_ref, ua_ref, dt_ref):
        n = ref.shape[-1]
        ref[...] = jnp.dot(xb, w_ref[:, off:off + n], preferred_element_type=F32).astype(ref.dtype)
        off += n


def _in_proj(h2d, norm_w, w_perm, tm):
    t = h2d.shape[0]
    widths = (SSD_CONV_CH, 4 * RWKV_WIDTH, N_BRANCH * D_MODEL, SSD_WIDTH, S5_WIDTH, DT_PAD)
    dtypes = (F32, F32, BF16, BF16, F32, F32)
    n_all = sum(widths)
    return pl.pallas_call(
        _in_proj_kernel,
        grid=(t // tm,),
        in_specs=[pl.BlockSpec((tm, D_MODEL), lambda i: (i, 0)),
                  _const_spec((1, D_MODEL)),
                  pl.BlockSpec((D_MODEL, n_all), lambda i: (0, 0), pipeline_mode=pl.Buffered(1))],
        out_specs=[pl.BlockSpec((tm, n), lambda i: (i, 0)) for n in widths],
        out_shape=[jax.ShapeDtypeStruct((t, n), dt) for n, dt in zip(widths, dtypes)],
        compiler_params=_params("parallel"),
        name="in_proj",
    )(h2d, norm_w.reshape(1, D_MODEL), w_perm)


def _permute_w_in(w):
    o_u, o_z, o_xbc = 0, S5_WIDTH, S5_WIDTH + SSD_WIDTH
    o_dt = o_xbc + SSD_CONV_CH
    o_rkvx = o_dt + SSD_HEADS
    o_g = o_rkvx + 4 * RWKV_WIDTH
    dt_cols = jnp.pad(w[:, o_dt:o_rkvx], ((0, 0), (0, DT_PAD - SSD_HEADS)))
    return jnp.concatenate([w[:, o_xbc:o_dt], w[:, o_rkvx:o_g], w[:, o_g:], w[:, o_z:o_xbc],
                            w[:, o_u:o_z], dt_cols], axis=1).astype(BF16)


def _s5_tables(lam_re, lam_im, log_step, b_re, b_im, c_re, c_im):
    q, g, n, hh = S5_Q, S5_GROUPS, S5_STATE, S5_GROUP
    step = jnp.exp(log_step)[:, :, None]
    j = jnp.arange(q + 1, dtype=F32)[:, None, None, None]
    mag = jnp.exp(lam_re * step * j)
    ang = lam_im * step * j
    pw_re, pw_im = mag * jnp.cos(ang), mag * jnp.sin(ang)
    ab_re, ab_im = pw_re[1], pw_im[1]
    den = lam_re * lam_re + lam_im * lam_im
    co_re = ((ab_re - 1.0) * lam_re + ab_im * lam_im) / den
    co_im = (ab_im * lam_re - (ab_re - 1.0) * lam_im) / den
    eb_re = co_re[..., None] * b_re - co_im[..., None] * b_im
    eb_im = co_re[..., None] * b_im + co_im[..., None] * b_re
    cp_re = c_re[None, None] * pw_re[:, :, :, None, :] - c_im[None, None] * pw_im[:, :, :, None, :]
    cp_im = c_re[None, None] * pw_im[:, :, :, None, :] + c_im[None, None] * pw_re[:, :, :, None, :]
    kern = (jnp.einsum('jdgon,dgni->jdgoi', cp_re, eb_re, precision=HI)
            - jnp.einsum('jdgon,dgni->jdgoi', cp_im, eb_im, precision=HI))
    kf, kb = kern[:q, 0], kern[:q, 1]
    lag = jnp.arange(q)[None, :] - jnp.arange(q)[:, None]
    alag = jnp.abs(lag)
    sel = lag[:, :, None, None, None]
    tst = jnp.where(sel > 0, kf[alag], jnp.where(sel < 0, kb[alag], (kf[0] + kb[0])[None, None]))
    toe = jnp.transpose(tst, (2, 0, 4, 1, 3)).reshape(g, q * hh, q * hh)

    def cmul(ar, ai, br, bi):
        return ar * br - ai * bi, ar * bi + ai * br

    pf_re, pf_im = pw_re[q - 1::-1, 0][:q], pw_im[q - 1::-1, 0][:q]
    pb_re, pb_im = pw_re[:q, 1], pw_im[:q, 1]
    sf_re, sf_im = cmul(pf_re[..., None], pf_im[..., None], eb_re[0][None], eb_im[0][None])
    sb_re, sb_im = cmul(pb_re[..., None], pb_im[..., None], eb_re[1][None], eb_im[1][None])
    so = jnp.stack([sf_re, sf_im, sb_re, sb_im], 0)
    so = jnp.transpose(so, (2, 1, 4, 0, 3)).reshape(g, q * hh, 4, n)
    rf_re, rf_im = cp_re[1:q + 1, 0], -cp_im[1:q + 1, 0]
    rb_re, rb_im = cp_re[q:0:-1, 1], -cp_im[q:0:-1, 1]
    ri = jnp.stack([rf_re, rf_im, rb_re, rb_im], 0)
    ri = jnp.transpose(ri, (2, 0, 4, 1, 3)).reshape(g, 4, n, q * hh)

    eye2 = jnp.eye(2, dtype=F32)
    npair = S5_PAIRS
    t2 = jnp.einsum('pgrc,gh->pgrhc', toe.reshape(npair, 2, q * hh, q * hh), eye2)
    t2 = t2.reshape(npair, 2 * q * hh, 2 * q * hh)
    so2 = jnp.einsum('pgrkn,gh->pgrkhn', so.reshape(npair, 2, q * hh, 4, n), eye2)
    so2 = so2.reshape(npair, 2 * q * hh, 4 * 2 * n)
    r2 = jnp.einsum('pgknr,gh->pkgnhr', ri.reshape(npair, 2, 4, n, q * hh), eye2)
    r2 = r2.reshape(npair, 4 * 2 * n, 2 * q * hh)
    apow = jnp.stack([pw_re[q, 0], pw_im[q, 0], pw_re[q, 1], pw_im[q, 1]], 0)
    apow = jnp.transpose(apow.reshape(4, npair, 2, n), (1, 0, 2, 3)).reshape(1, npair * 4 * 2 * n)
    return t2, so2, r2, apow


def _s5_send_kernel(z_ref, so_ref, send_ref):
    send_ref[0] = _mm(z_ref[0], so_ref[0])


def _s5_scan_kernel(send_ref, apow_ref, hin_ref):
    nc = send_ref.shape[1]
    lanes = 2 * S5_STATE
    apow = apow_ref[...]

    def piece(row, p, k):
        o = (p * 4 + k) * lanes
        return row[:, o:o + lanes]

    sub = _iota((HALO, lanes), 0)

    def body(i, carry):
        starts = (pl.multiple_of(i * HALO, HALO), pl.multiple_of(nc - HALO - i * HALO, HALO))
        tiles = tuple(send_ref[0, pl.ds(s, HALO), :] for s in starts)
        new = []
        for p in range(S5_PAIRS):
            for d in range(2):
                h_re, h_im = carry[(p * 2 + d) * 2], carry[(p * 2 + d) * 2 + 1]
                a_re, a_im = piece(apow, p, 2 * d), piece(apow, p, 2 * d + 1)
                s_re, s_im = piece(tiles[d], p, 2 * d), piece(tiles[d], p, 2 * d + 1)
                out_re = jnp.zeros((HALO, lanes), F32)
                out_im = jnp.zeros((HALO, lanes), F32)
                for step in range(HALO):
                    rix = HALO - 1 - step if d else step
                    out_re = jnp.where(sub == rix, h_re, out_re)
                    out_im = jnp.where(sub == rix, h_im, out_im)
                    n_re = a_re * h_re - a_im * h_im + s_re[rix:rix + 1, :]
                    n_im = a_re * h_im + a_im * h_re + s_im[rix:rix + 1, :]
                    h_re, h_im = n_re, n_im
                o_re = (p * 4 + 2 * d) * lanes
                hin_ref[0, pl.ds(starts[d], HALO), o_re:o_re + lanes] = out_re
                hin_ref[0, pl.ds(starts[d], HALO), o_re + lanes:o_re + 2 * lanes] = out_im
                new += [h_re, h_im]
        return tuple(new)

    zero = jnp.zeros((1, lanes), F32)
    lax.fori_loop(0, nc // HALO, body, (zero,) * (S5_PAIRS * 4))


def _s5_out_kernel(z_ref, hin_ref, t2_ref, r2_ref, y_ref):
    y_ref[0] = _mm(z_ref[0], t2_ref[0]) + _mm(hin_ref[0], r2_ref[0])


def _s5_mixer_core(u, tables):
    t2, so2, r2, apow = tables
    bsz, lp, _ = u.shape
    nc = lp // S5_Q
    width = S5_GROUPS * S5_Q * S5_GROUP
    pw = width // S5_PAIRS
    z = jnp.transpose(u.reshape(bsz, nc, S5_Q, S5_GROUPS, S5_GROUP), (0, 1, 3, 2, 4)).reshape(bsz, nc, width)
    blk = pl.BlockSpec((1, nc, pw), lambda p, b: (b, 0, p))
    wblk = pl.BlockSpec((1, pw, pw), lambda p, b: (p, 0, 0))
    send = pl.pallas_call(
        _s5_send_kernel, grid=(S5_PAIRS, bsz), in_specs=[blk, wblk], out_specs=blk,
        out_shape=jax.ShapeDtypeStruct((bsz, nc, width), F32),
        compiler_params=_params("parallel", "parallel"), name="s5_send")(z, so2)
    row = pl.BlockSpec((1, nc, width), lambda b: (b, 0, 0))
    hin = pl.pallas_call(
        _s5_scan_kernel, grid=(bsz,), in_specs=[row, _const_spec((1, width))], out_specs=row,
        out_shape=jax.ShapeDtypeStruct((bsz, nc, width), F32),
        compiler_params=_params("parallel"), name="s5_scan")(send, apow)
    y = pl.pallas_call(
        _s5_out_kernel, grid=(S5_PAIRS, bsz), in_specs=[blk, blk, wblk, wblk], out_specs=blk,
        out_shape=jax.ShapeDtypeStruct((bsz, nc, width), F32),
        compiler_params=_params("parallel", "parallel"), name="s5_out")(z, hin, t2, r2)
    y = jnp.transpose(y.reshape(bsz, nc, S5_GROUPS, S5_Q, S5_GROUP), (0, 1, 3, 2, 4))
    return y.reshape(bsz, lp, S5_WIDTH)


def _ssd_kernel(cur_ref, prev_ref, next_ref, dt_ref, cw_ref, cb_ref, dtb_ref, aneg_ref, dskip_ref,
                y_ref, ext_ref, h_ref, *, reverse, front, add_skip):
    c = pl.program_id(1)
    nc = pl.num_programs(1)
    cm = nc - 1 - c if reverse else c
    q = SSD_CHUNK

    @pl.when(c == 0)
    def _():
        h_ref[...] = jnp.zeros_like(h_ref)

    has_prev = jnp.where(cm > 0, 1.0, 0.0)
    has_next = jnp.where(cm < nc - 1, 1.0, 0.0)
    ext_ref[0:HALO, :] = prev_ref[0] * has_prev
    ext_ref[HALO:HALO + q, :] = cur_ref[0]
    ext_ref[HALO + q:HALO + q + HALO, :] = next_ref[0] * has_next
    acc = jnp.zeros((q, SSD_CONV_CH), F32) + cb_ref[...]
    pad = SSD_CONV // 2
    for k in range(SSD_CONV):
        acc = acc + cw_ref[k:k + 1, :] * ext_ref[HALO - pad + k:HALO - pad + k + q, :]
    valid = (cm * q + _iota((q, 1), 0)) >= front
    xc = jnp.where(valid, acc * jax.nn.sigmoid(acc), 0.0)
    x = xc[:, :SSD_WIDTH]
    bm = xc[:, SSD_WIDTH:SSD_WIDTH + SSD_GROUPS * SSD_STATE]
    cmat = xc[:, SSD_WIDTH + SSD_GROUPS * SSD_STATE:]

    dt = jnp.where(valid, jax.nn.softplus(dt_ref[0] + dtb_ref[...]), 0.0)
    a = dt * aneg_ref[...]
    rr, cc = _iota((q, q), 0), _iota((q, q), 1)
    tri = (cc >= rr) if reverse else (cc <= rr)
    cs = _mm_hi(tri.astype(F32), a)
    cs_t = cs.T
    tot = cs[0:1, :] if reverse else cs[q - 1:q, :]
    e_cs = jnp.exp(cs)
    to_end = jnp.exp(tot - cs)
    chunk_decay = jnp.exp(tot)

    for g in range(SSD_GROUPS):
        bg = bm[:, g * SSD_STATE:(g + 1) * SSD_STATE]
        cg = cmat[:, g * SSD_STATE:(g + 1) * SSD_STATE]
        cb = _mm_nt(cg, bg)
        for jj in range(SSD_HPG):
            j = g * SSD_HPG + jj
            xh = x[:, j * SSD_HEAD_DIM:(j + 1) * SSD_HEAD_DIM]
            xdt = xh * dt[:, j:j + 1]
            seg = jnp.exp(jnp.where(tri, cs[:, j:j + 1] - cs_t[j:j + 1, :], -jnp.inf))
            y = _mm(cb * seg, xdt)
            h_in = h_ref[j]
            y = y + _mm(cg, h_in) * e_cs[:, j:j + 1]
            st = _mm((bg * to_end[:, j:j + 1]).T, xdt)
            h_ref[j] = h_in * chunk_decay[:, j:j + 1] + st
            if add_skip:
                y = y + dskip_ref[:, j * SSD_HEAD_DIM:(j + 1) * SSD_HEAD_DIM] * xh
            y_ref[0, :, j * SSD_HEAD_DIM:(j + 1) * SSD_HEAD_DIM] = y


def _ssd_direction(xbc, dt_raw, conv_w, conv_b, dt_bias, a_log, d_skip, *, reverse, front, add_skip):
    bsz, lp, _ = xbc.shape
    nc = lp // SSD_CHUNK
    per = SSD_CHUNK // HALO
    nh = lp // HALO

    def cm(c):
        return nc - 1 - c if reverse else c

    cur = pl.BlockSpec((1, SSD_CHUNK, SSD_CONV_CH), lambda b, c: (b, cm(c), 0))
    prv = pl.BlockSpec((1, HALO, SSD_CONV_CH), lambda b, c: (b, jnp.maximum(cm(c) * per - 1, 0), 0))
    nxt = pl.BlockSpec((1, HALO, SSD_CONV_CH), lambda b, c: (b, jnp.minimum(cm(c) * per + per, nh - 1), 0))
    dts = pl.BlockSpec((1, SSD_CHUNK, DT_PAD), lambda b, c: (b, cm(c), 0))
    cw = jnp.pad(conv_w, ((0, HALO - SSD_CONV), (0, 0)))
    dtb = jnp.pad(dt_bias, (0, DT_PAD - SSD_HEADS)).reshape(1, DT_PAD)
    aneg = jnp.pad(-jnp.exp(a_log), (0, DT_PAD - SSD_HEADS)).reshape(1, DT_PAD)
    dsk = jnp.repeat(d_skip, SSD_HEAD_DIM).reshape(1, SSD_WIDTH)
    kern = functools.partial(_ssd_kernel, reverse=reverse, front=front, add_skip=add_skip)
    return pl.pallas_call(
        kern, grid=(bsz, nc),
        in_specs=[cur, prv, nxt, dts, _const_spec((HALO, SSD_CONV_CH)), _const_spec((1, SSD_CONV_CH)),
                  _const_spec((1, DT_PAD)), _const_spec((1, DT_PAD)), _const_spec((1, SSD_WIDTH))],
        out_specs=pl.BlockSpec((1, SSD_CHUNK, SSD_WIDTH), lambda b, c: (b, cm(c), 0)),
        out_shape=jax.ShapeDtypeStruct((bsz, lp, SSD_WIDTH), F32),
        scratch_shapes=[pltpu.VMEM((SSD_CHUNK + 2 * HALO, SSD_CONV_CH), F32),
                        pltpu.VMEM((SSD_HEADS, SSD_STATE, SSD_HEAD_DIM), F32)],
        compiler_params=_params("parallel", "arbitrary"),
        name="ssd_bwd" if reverse else "ssd_fwd",
    )(xbc, xbc, xbc, dt_raw, cw, conv_b.reshape(1, SSD_CONV_CH), dtb, aneg, dsk)


def _head_ones():
    rr, cc = _iota((RWKV_WIDTH, RWKV_WIDTH), 0), _iota((RWKV_WIDTH, RWKV_WIDTH), 1)
    return (jnp.right_shift(rr, 6) == jnp.right_shift(cc, 6)).astype(F32)


def _rwkv_pre_kernel(cur_ref, prev_ref, next_ref, mu_ref, vec_ref, g1_ref, g2_ref, w1_ref, w2_ref, a1_ref,
                     a2_ref, w0a0_ref,
                     r_ref, v_ref, nkk_ref, g_ref, bonus_ref, lw0_ref, kd0_ref, b0_ref, lw1_ref, kd1_ref,
                     b1_ref, *, front):
    j = pl.program_id(1)
    nb = pl.num_programs(1)
    tb = cur_ref.shape[1]
    w = RWKV_WIDTH
    cur = cur_ref[0]
    prow = prev_ref[0, HALO - 1:HALO, :] * jnp.where(j > 0, 1.0, 0.0)
    nrow = next_ref[0, 0:1, :] * jnp.where(j < nb - 1, 1.0, 0.0)
    rowi = _iota((tb, 1), 0)
    prev = jnp.where(rowi == 0, prow, pltpu.roll(cur, 1, 0))
    nxt = jnp.where(rowi == tb - 1, nrow, pltpu.roll(cur, tb - 1, 0))
    shift = 0.5 * (prev + nxt) - cur
    valid = (j * tb + rowi) >= front
    r = cur[:, 0:w] + shift[:, 0:w] * mu_ref[0:1, :]
    k = jnp.where(valid, cur[:, w:2 * w] + shift[:, w:2 * w] * mu_ref[1:2, :], 0.0)
    v = jnp.where(valid, cur[:, 2 * w:3 * w] + shift[:, 2 * w:3 * w] * mu_ref[2:3, :], 0.0)
    xc, dxc = cur[:, 3 * w:], shift[:, 3 * w:]
    xw = xc + dxc * mu_ref[3:4, :]
    xa = xc + dxc * mu_ref[4:5, :]
    xg = xc + dxc * mu_ref[5:6, :]
    k_k, k_a, r_k = vec_ref[0:1, :], vec_ref[1:2, :], vec_ref[2:3, :]
    ones = _head_ones()
    g_ref[0] = _mm(jax.nn.sigmoid(_mm(xg, g1_ref[...])), g2_ref[...])
    kk = k * k_k
    kk = kk * lax.rsqrt(_mm_split(kk * kk, ones) + 1e-12)
    r_ref[0] = r
    v_ref[0] = v
    nkk_ref[0] = -kk
    bonus_ref[0] = _mm_split(r * k * r_k, ones) * v
    for d, (lw_ref, kd_ref, b_ref) in enumerate(((lw0_ref, kd0_ref, b0_ref), (lw1_ref, kd1_ref, b1_ref))):
        lw = w0a0_ref[d:d + 1, :] + _mm(jnp.tanh(_mm(xw, w1_ref[d])), w2_ref[d])
        w_log = -jax.nn.softplus(-lw) - 0.5
        lw_ref[0] = -jnp.exp(w_log)
        ag = jax.nn.sigmoid(w0a0_ref[2 + d:3 + d, :] + _mm(_mm(xa, a1_ref[d]), a2_ref[d]))
        kd_ref[0] = k * (1.0 + (ag - 1.0) * k_a)
        b_ref[0] = kk * ag


def _rwkv_pre(rkvx, mu, vec, g1, g2, w1, w2, a1, a2, w0a0, *, front, tb):
    bsz, lp, _ = rkvx.shape
    per = tb // HALO
    nh = lp // HALO
    w = RWKV_WIDTH
    cur = pl.BlockSpec((1, tb, 4 * w), lambda b, j: (b, j, 0))
    prv = pl.BlockSpec((1, HALO, 4 * w), lambda b, j: (b, jnp.maximum(j * per - 1, 0), 0))
    nxt = pl.BlockSpec((1, HALO, 4 * w), lambda b, j: (b, jnp.minimum(j * per + per, nh - 1), 0))
    out = pl.BlockSpec((1, tb, w), lambda b, j: (b, j, 0))
    consts = [mu, vec, g1, g2, w1, w2, a1, a2, w0a0]
    return pl.pallas_call(
        functools.partial(_rwkv_pre_kernel, front=front), grid=(bsz, lp // tb),
        in_specs=[cur, prv, nxt] + [_const_spec(c.shape) for c in consts],
        out_specs=[out] * 11,
        out_shape=[jax.ShapeDtypeStruct((bsz, lp, w), F32)] * 11,
        compiler_params=_params("parallel", "parallel"),
        name="rwkv_pre",
    )(rkvx, rkvx, rkvx, *consts)


def _rwkv_chunk_ops(chains):
    n = RWKV_CHUNK
    w = RWKV_WIDTH
    lane_head = jnp.right_shift(_iota((n, w), 1), 6)

    def stack(x):
        return jnp.concatenate([jnp.where(lane_head == h, x, 0.0) for h in range(RWKV_HEADS)], axis=0)

    rr, cc = _iota((w, w), 0), _iota((w, w), 1)
    same = jnp.right_shift(rr, 6) == jnp.right_shift(cc, 6)
    tl, sl = jnp.bitwise_and(rr, n - 1), jnp.bitwise_and(cc, n - 1)
    eye = (rr == cc).astype(F32)
    r8, c8 = _iota((n, n), 0), _iota((n, n), 1)
    masks = {}
    for rev in sorted({ch[6] for ch in chains}):
        masks[rev] = (same & ((sl > tl) if rev else (sl < tl)), same & ((sl >= tl) if rev else (sl <= tl)),
                      ((c8 >= r8) if rev else (c8 <= r8)).astype(F32))

    pre = []
    for r, v, a, kd, b, lw, rev in chains:
        cum = _mm_hi(masks[rev][2], lw)
        tot = cum[0:1, :] if rev else cum[n - 1:n, :]
        p_inv = jnp.exp(-cum)
        p_end = jnp.exp(tot - cum)
        r_t = stack(r * jnp.exp(cum))
        a_t = stack(a * jnp.exp(cum - lw))
        lhs = jnp.concatenate([a_t, r_t], axis=0)
        rhs = jnp.concatenate([stack(b * p_inv), stack(kd * p_inv)], axis=0)
        pre.append((lhs, rhs, r_t, a_t, stack(b * p_end).T, stack(kd * p_end).T, stack(v), jnp.exp(tot)))

    grams = [_mm_nt(p[0], p[1]) for p in pre]
    parts = []
    for ch, gram in zip(chains, grams):
        strict, incl, _ = masks[ch[6]]
        parts.append((jnp.where(strict, gram[0:w, 0:w], 0.0), jnp.where(strict, gram[0:w, w:2 * w], 0.0),
                      jnp.where(incl, gram[w:2 * w, 0:w], 0.0), jnp.where(incl, gram[w:2 * w, w:2 * w], 0.0)))

    def same_block(shift):
        return jnp.right_shift(rr, shift) == jnp.right_shift(cc, shift)

    invs = [eye + jnp.where(same_block(1), p[0], 0.0) for p in parts]
    on_v = [_mm(jnp.concatenate([p[1], p[3], q[5]], axis=0), q[6]) for p, q in zip(parts, pre)]
    for shift in range(1, RWKV_CHUNK.bit_length() - 1):
        between = same_block(shift + 1) & (jnp.right_shift(rr, shift) != jnp.right_shift(cc, shift))
        ys = [_mm(jnp.where(between, p[0], 0.0), inv) for p, inv in zip(parts, invs)]
        invs = [inv + _mm(inv, y) for inv, y in zip(invs, ys)]
    wus = [_mm(inv, jnp.concatenate([q[3], ov[0:w]], axis=1))
           for inv, q, ov in zip(invs, pre, on_v)]
    on_wu = [_mm(jnp.concatenate([p[2], q[4]], axis=0), wu) for p, q, wu in zip(parts, pre, wus)]
    out = []
    for q, ov, ow in zip(pre, on_v, on_wu):
        rp = q[2] + ow[0:w, 0:w]
        y0 = ow[0:w, w:2 * w] + ov[w:2 * w]
        mk = eye * q[7] + ow[w:2 * w, 0:w]
        nk = ow[w:2 * w, w:2 * w] + ov[2 * w:3 * w]
        out.append((y0, rp, mk, nk))
    return out


def _rwkv_scan_kernel(rf_ref, vf_ref, af_ref, lwf_ref, kdf_ref, bf_ref,
                      rb_ref, vb_ref, ab_ref, lwb_ref, kdb_ref, bb_ref,
                      yf_ref, yb_ref, sf_ref, sb_ref, *, group):
    n = RWKV_CHUNK

    @pl.when(pl.program_id(1) == 0)
    def _():
        sf_ref[...] = jnp.zeros_like(sf_ref)
        sb_ref[...] = jnp.zeros_like(sb_ref)

    def unstack(x):
        out = x[0:n]
        for h in range(1, RWKV_HEADS):
            out = out + x[h * n:(h + 1) * n]
        return out

    dirs = (((rf_ref, vf_ref, af_ref, kdf_ref, bf_ref, lwf_ref), yf_ref, sf_ref, False),
            ((rb_ref, vb_ref, ab_ref, kdb_ref, bb_ref, lwb_ref), yb_ref, sb_ref, True))
    chains = []
    for step in range(group):
        for refs, _, _, reverse in dirs:
            gi = group - 1 - step if reverse else step
            chains.append(tuple(ref[0, gi * n:(gi + 1) * n, :] for ref in refs) + (reverse,))
    ops = _rwkv_chunk_ops(chains)
    states = [s_ref[...] for _, _, s_ref, _ in dirs]
    for step in range(group):
        for d, (_, y_ref, _, reverse) in enumerate(dirs):
            gi = group - 1 - step if reverse else step
            y0, rp, mk, nk = ops[step * len(dirs) + d]
            on_st = _mm(jnp.concatenate([rp, mk], axis=0), states[d])
            y_ref[0, gi * n:(gi + 1) * n, :] = unstack(y0 + on_st[0:RWKV_WIDTH])
            states[d] = on_st[RWKV_WIDTH:] + nk
    for d, (_, _, s_ref, _) in enumerate(dirs):
        s_ref[...] = states[d]


def _rwkv_scan(r, v, nkk, lw0, kd0, b0, lw1, kd1, b1, *, group):
    bsz, lp, w = r.shape
    rows = group * RWKV_CHUNK
    nb = lp // rows
    fwd = pl.BlockSpec((1, rows, w), lambda bb, c: (bb, c, 0))
    bwd = pl.BlockSpec((1, rows, w), lambda bb, c: (bb, nb - 1 - c, 0))
    return pl.pallas_call(
        functools.partial(_rwkv_scan_kernel, group=group), grid=(bsz, nb),
        in_specs=[fwd] * 6 + [bwd] * 6, out_specs=[fwd, bwd],
        out_shape=[jax.ShapeDtypeStruct((bsz, lp, w), F32)] * 2,
        scratch_shapes=[pltpu.VMEM((w, w), F32)] * 2,
        compiler_params=_params("parallel", "arbitrary"),
        name="rwkv_scan",
    )(r, v, nkk, lw0, kd0, b0, r, v, nkk, lw1, kd1, b1)


def _merge_kernel(h_ref, gates_ref, ys5_ref, ua_ref, ysf_ref, ysb_ref, z_ref, ycf_ref, ycb_ref, g_ref,
                  bonus_ref, vec_a_ref, glu_w_ref, glu_b_ref, ssd_nw_ref, ln_ref, pa_ref, pb_ref, pc_ref,
                  wo_ref, o_ref):
    ua = ua_ref[...]
    ya = jax.nn.gelu(ys5_ref[...] + vec_a_ref[...] * ua)
    zz = _mm(ya, glu_w_ref[...]) + glu_b_ref[...]
    ya = zz[:, :S5_WIDTH] * jax.nn.sigmoid(zz[:, S5_WIDTH:])
    z = z_ref[...].astype(F32)
    yb = (ysf_ref[...] + ysb_ref[...]) * (z * jax.nn.sigmoid(z))
    yb = yb * lax.rsqrt(jnp.mean(yb * yb, -1, keepdims=True) + EPS) * ssd_nw_ref[...]
    yc = ycf_ref[...] + ycb_ref[...]
    avg = _head_ones() * (1.0 / RWKV_HEAD)
    mean = _mm_split(yc, avg)
    dev = yc - mean
    var = _mm_split(dev * dev, avg)
    yc = dev * lax.rsqrt(var + RWKV_LN_EPS) * ln_ref[0:1, :] + ln_ref[1:2, :]
    yc = (yc + bonus_ref[...]) * g_ref[...]
    d = D_MODEL
    merged = (jax.nn.sigmoid(gates_ref[:, 0:d].astype(F32)) * _mm(ya, pa_ref[...])
              + jax.nn.sigmoid(gates_ref[:, d:2 * d].astype(F32)) * _mm(yb, pb_ref[...])
              + jax.nn.sigmoid(gates_ref[:, 2 * d:3 * d].astype(F32)) * _mm(yc, pc_ref[...]))
    o_ref[...] = h_ref[...] + _mm(merged, wo_ref[...])


def _merge(h2d, gates, ys5, ua, ysf, ysb, z, ycf, ycb, g, bonus, consts, tm):
    t = h2d.shape[0]
    toks = [h2d, gates, ys5, ua, ysf, ysb, z, ycf, ycb, g, bonus]
    return pl.pallas_call(
        _merge_kernel, grid=(t // tm,),
        in_specs=[pl.BlockSpec((tm, a.shape[1]), lambda i: (i, 0)) for a in toks]
        + [_const_spec(c.shape) for c in consts],
        out_specs=pl.BlockSpec((tm, D_MODEL), lambda i: (i, 0)),
        out_shape=jax.ShapeDtypeStruct((t, D_MODEL), F32),
        compiler_params=_params("parallel"),
        name="merge",
    )(*toks, *consts)


def _mlp_kernel(h_ref, nw_ref, w1_ref, w2_ref, fw_ref, o_ref, *, front, rows_per_batch, final):
    x = h_ref[...]
    tm = x.shape[0]
    xn = x * lax.rsqrt(jnp.mean(x * x, -1, keepdims=True) + EPS) * nw_ref[...]
    a = jnp.maximum(jnp.dot(xn.astype(BF16), w1_ref[...], preferred_element_type=F32), 0.0)
    y = x + jnp.dot((a * a).astype(BF16), w2_ref[...], preferred_element_type=F32)
    if final:
        y = y * lax.rsqrt(jnp.mean(y * y, -1, keepdims=True) + EPS) * fw_ref[...]
    row = (pl.program_id(0) * tm) % rows_per_batch + _iota((tm, 1), 0)
    row = jnp.where(row >= rows_per_batch, row - rows_per_batch, row)
    o_ref[...] = jnp.where(row >= front, y, 0.0)


def _mlp(h2d, norm_w, w1, w2, final_w, *, front, rows_per_batch, final, tm):
    t = h2d.shape[0]
    kern = functools.partial(_mlp_kernel, front=front, rows_per_batch=rows_per_batch, final=final)
    return pl.pallas_call(
        kern, grid=(t // tm,),
        in_specs=[pl.BlockSpec((tm, D_MODEL), lambda i: (i, 0)), _const_spec((1, D_MODEL)),
                  pl.BlockSpec((D_MODEL, D_FF), lambda i: (0, 0), pipeline_mode=pl.Buffered(1)),
                  pl.BlockSpec((D_FF, D_MODEL), lambda i: (0, 0), pipeline_mode=pl.Buffered(1)),
                  _const_spec((1, D_MODEL))],
        out_specs=pl.BlockSpec((tm, D_MODEL), lambda i: (i, 0)),
        out_shape=jax.ShapeDtypeStruct((t, D_MODEL), F32),
        compiler_params=_params("parallel"),
        name="mlp",
    )(h2d, norm_w.reshape(1, D_MODEL), w1.astype(BF16), w2.astype(BF16), final_w.reshape(1, D_MODEL))


def _token_tile(t, cap):
    tm = cap
    while t % tm:
        tm //= 2
    return tm


def _row_tile(lp, cap):
    k = lp // ROW_ALIGN
    best = 1
    for m in range(1, k + 1):
        if k % m == 0 and m * ROW_ALIGN <= cap:
            best = m
    return best * ROW_ALIGN


def kernel(x, meta_tokens, final_norm_w, mix_norm_w, w_in, s5_lambda_re, s5_lambda_im, s5_log_step, s5_b_re, s5_b_im, s5_c_re, s5_c_im, s5_d, s5_glu_w, s5_glu_b, ssd_conv_w, ssd_conv_b, ssd_a_log, ssd_dt_bias, ssd_d, ssd_norm_w, rwkv_mu_rkv, rwkv_mu_wag, rwkv_w0, rwkv_w1, rwkv_w2, rwkv_a0, rwkv_a1, rwkv_a2, rwkv_g1, rwkv_g2, rwkv_k_k, rwkv_k_a, rwkv_r_k, rwkv_ln_w, rwkv_ln_b, proj_a, proj_b, proj_c, w_out, mlp_norm_w, mlp_w1, mlp_w2):
    bsz, seq, d = x.shape
    assert d == D_MODEL
    length = N_META + seq
    lp = -(-length // ROW_ALIGN) * ROW_ALIGN
    front = lp - length
    t = bsz * lp
    depth = w_in.shape[0]
    meta = jnp.broadcast_to(meta_tokens[None].astype(x.dtype), (bsz, N_META, d))
    h = jnp.concatenate([jnp.zeros((bsz, front, d), x.dtype), meta, x], axis=1).reshape(t, d)
    tm = _token_tile(t, 512)
    tb = _row_tile(lp, 512)
    rwkv_group = 2

    for i in range(depth):
        xbc, rkvx, gates, z, ua, dt_raw = _in_proj(h, mix_norm_w[i], _permute_w_in(w_in[i]), tm)
        tables = _s5_tables(s5_lambda_re[i], s5_lambda_im[i], s5_log_step[i], s5_b_re[i], s5_b_im[i],
                            s5_c_re[i], s5_c_im[i])
        ys5 = _s5_mixer_core(ua.reshape(bsz, lp, S5_WIDTH), tables).reshape(t, S5_WIDTH)
        xbc3 = xbc.reshape(bsz, lp, SSD_CONV_CH)
        dt3 = dt_raw.reshape(bsz, lp, DT_PAD)
        ys = [_ssd_direction(xbc3, dt3, ssd_conv_w[i], ssd_conv_b[i], ssd_dt_bias[i, dr], ssd_a_log[i, dr],
                             ssd_d[i], reverse=bool(dr), front=front, add_skip=(dr == 0)).reshape(t, SSD_WIDTH)
              for dr in range(2)]
        mu = jnp.concatenate([rwkv_mu_rkv[i], rwkv_mu_wag[i]], axis=0)
        vec = jnp.stack([rwkv_k_k[i], rwkv_k_a[i], rwkv_r_k[i].reshape(RWKV_WIDTH)], axis=0)
        w0a0 = jnp.concatenate([rwkv_w0[i], rwkv_a0[i]], axis=0)
        pre = _rwkv_pre(rkvx.reshape(bsz, lp, 4 * RWKV_WIDTH), mu, vec, rwkv_g1[i], rwkv_g2[i], rwkv_w1[i],
                        rwkv_w2[i], rwkv_a1[i], rwkv_a2[i], w0a0, front=front, tb=tb)
        r, v, nkk, g, bonus, lw0, kd0, b0, lw1, kd1, b1 = pre
        ycf, ycb = _rwkv_scan(r, v, nkk, lw0, kd0, b0, lw1, kd1, b1, group=rwkv_group)
        ycf, ycb = ycf.reshape(t, RWKV_WIDTH), ycb.reshape(t, RWKV_WIDTH)
        consts = [s5_d[i].reshape(1, S5_WIDTH), s5_glu_w[i].astype(BF16), s5_glu_b[i].reshape(1, 2 * S5_WIDTH),
                  ssd_norm_w[i].reshape(1, SSD_WIDTH), jnp.stack([rwkv_ln_w[i], rwkv_ln_b[i]], axis=0),
                  proj_a[i].astype(BF16), proj_b[i].astype(BF16), proj_c[i].astype(BF16),
                  w_out[i].astype(BF16)]
        h = _merge(h, gates, ys5, ua, ys[0], ys[1], z, ycf, ycb, g.reshape(t, RWKV_WIDTH),
                   bonus.reshape(t, RWKV_WIDTH), consts, tm)
        h = _mlp(h, mlp_norm_w[i], mlp_w1[i], mlp_w2[i], final_norm_w, front=front, rows_per_batch=lp,
                 final=(i == depth - 1), tm=tm)
    return h.reshape(bsz, lp, d)[:, front + N_META:]
```

```python
import functools

import jax
import jax.numpy as jnp
from jax import lax
from jax.experimental import pallas as pl
from jax.experimental.pallas import tpu as pltpu

D_MODEL = 1024
N_META = 16
EPS = 1e-6
D_FF = 4 * D_MODEL

S5_WIDTH = 256
S5_GROUP = 16
S5_GROUPS = 16
S5_STATE = 64
S5_Q = 16

SSD_WIDTH = 512
SSD_HEAD_DIM = 64
SSD_HEADS = 8
SSD_GROUPS = 2
SSD_HPG = SSD_HEADS // SSD_GROUPS
SSD_STATE = 128
SSD_CONV = 5
SSD_CHUNK = 128
SSD_CONV_CH = SSD_WIDTH + 2 * SSD_GROUPS * SSD_STATE

RWKV_WIDTH = 256
RWKV_HEAD = 64
RWKV_HEADS = 4
RWKV_LN_EPS = 64e-5
RWKV_CHUNK = 64

N_BRANCH = 3
DT_PAD = 128
ROW_ALIGN = 128
HALO = 8

F32 = jnp.float32
BF16 = jnp.bfloat16
VMEM_LIMIT = 56 * 1024 * 1024


def _mm(a, b):
    return jnp.dot(a.astype(BF16), b.astype(BF16), preferred_element_type=F32)


def _bf16_parts(a, parts):
    out = []
    rest = a
    for _ in range(parts):
        part = rest.astype(BF16)
        rest = rest - part.astype(F32)
        out.append(part)
    return out


def _mm_split(a, b, parts=2):
    bb = b.astype(BF16)
    terms = [jnp.dot(p, bb, preferred_element_type=F32) for p in _bf16_parts(a, parts)]
    return functools.reduce(lambda x, y: x + y, terms)


def _mm_cumsum(tri, a):
    t = tri.astype(BF16)
    terms = [jnp.dot(t, p, preferred_element_type=F32) for p in _bf16_parts(a, 3)]
    return functools.reduce(lambda x, y: x + y, terms)


def _mm_nt(a, b):
    return lax.dot_general(a.astype(BF16), b.astype(BF16), (((1,), (1,)), ((), ())),
                           preferred_element_type=F32)


def _iota(shape, dim):
    return lax.broadcasted_iota(jnp.int32, shape, dim)


def _params(*sem):
    return pltpu.CompilerParams(dimension_semantics=sem, vmem_limit_bytes=VMEM_LIMIT)


def _const_spec(shape):
    nd = len(shape)
    return pl.BlockSpec(shape, lambda *_: (0,) * nd)


def _in_proj_kernel(h_ref, nw_ref, w_ref, xbc_ref, rkvx_ref, gates_ref, z_ref, ua_ref, dt_ref):
    x = h_ref[...]
    xn = x * lax.rsqrt(jnp.mean(x * x, -1, keepdims=True) + EPS) * nw_ref[...]
    xb = xn.astype(BF16)
    off = 0
    for ref in (xbc_ref, rkvx_ref, gates_ref, z_ref, ua_ref, dt_ref):
        n = ref.shape[-1]
        ref[...] = jnp.dot(xb, w_ref[:, off:off + n], preferred_element_type=F32).astype(ref.dtype)
        off += n


def _in_proj(h2d, norm_w, w_perm, tm):
    t = h2d.shape[0]
    widths = (SSD_CONV_CH, 4 * RWKV_WIDTH, N_BRANCH * D_MODEL, SSD_WIDTH, S5_WIDTH, DT_PAD)
    dtypes = (F32, F32, BF16, BF16, F32, F32)
    n_all = sum(widths)
    return pl.pallas_call(
        _in_proj_kernel,
        grid=(t // tm,),
        in_specs=[pl.BlockSpec((tm, D_MODEL), lambda i: (i, 0)),
                  _const_spec((1, D_MODEL)),
                  pl.BlockSpec((D_MODEL, n_all), lambda i: (0, 0), pipeline_mode=pl.Buffered(1))],
        out_specs=[pl.BlockSpec((tm, n), lambda i: (i, 0)) for n in widths],
        out_shape=[jax.ShapeDtypeStruct((t, n), dt) for n, dt in zip(widths, dtypes)],
        compiler_params=_params("parallel"),
        name="in_proj",
    )(h2d, norm_w.reshape(1, D_MODEL), w_perm)


def _permute_w_in(w):
    o_u, o_z, o_xbc = 0, S5_WIDTH, S5_WIDTH + SSD_WIDTH
    o_dt = o_xbc + SSD_CONV_CH
    o_rkvx = o_dt + SSD_HEADS
    o_g = o_rkvx + 4 * RWKV_WIDTH
    dt_cols = jnp.pad(w[:, o_dt:o_rkvx], ((0, 0), (0, DT_PAD - SSD_HEADS)))
    return jnp.concatenate([w[:, o_xbc:o_dt], w[:, o_rkvx:o_g], w[:, o_g:], w[:, o_z:o_xbc],
                            w[:, o_u:o_z], dt_cols], axis=1).astype(BF16)


def _s5_tables(lam_re, lam_im, log_step, b_re, b_im, c_re, c_im):
    g, n, hh = S5_GROUPS, S5_STATE, S5_GROUP
    step = jnp.exp(log_step)[:, :, None]

    def power(j):
        mag = jnp.exp(lam_re * step * j)
        return mag * jnp.cos(lam_im * step * j), mag * jnp.sin(lam_im * step * j)

    ab_re, ab_im = power(1.0)
    aq_re, aq_im = power(float(S5_Q))
    den = lam_re * lam_re + lam_im * lam_im
    co_re = ((ab_re - 1.0) * lam_re + ab_im * lam_im) / den
    co_im = (ab_im * lam_re - (ab_re - 1.0) * lam_im) / den
    eb_re = co_re[..., None] * b_re - co_im[..., None] * b_im
    eb_im = co_re[..., None] * b_im + co_im[..., None] * b_re
    eye = jnp.eye(g, dtype=F32)
    b_in = jnp.stack([jnp.einsum('dgni,gk->dgikn', e, eye) for e in (eb_re, eb_im)], axis=3)
    b_in = b_in.reshape(2, g * hh, 2 * g * n).astype(BF16)
    c_out = jnp.stack([jnp.einsum('gon,gk->gnko', c, eye) for c in (c_re, -c_im)], axis=0)
    c_out = c_out.reshape(2 * g * n, g * hh).astype(BF16)
    rows = [ab_re[0], ab_im[0], ab_re[1], ab_im[1], aq_re[0], aq_im[0], aq_re[1], aq_im[1]]
    return b_in, c_out, jnp.stack([r.reshape(g * n) for r in rows], axis=0)


def _s5_kernel(u_ref, bin_ref, cout_ref, a_ref, y_ref, h_ref, bnd_ref, us_ref, ys_ref):
    q = S5_Q
    nc = u_ref.shape[1] // q
    gn = S5_GROUPS * S5_STATE
    half = S5_WIDTH // 2
    for hh in range(2):
        us_ref[hh] = u_ref[0, :, hh * half:(hh + 1) * half]

    def token_rows(ref, hh, t):
        return ref.at[hh, pl.ds(t, nc, stride=q), :]

    def advance(i):
        for d, t in ((0, i), (1, q - 1 - i)):
            ut = jnp.concatenate([token_rows(us_ref, hh, t)[...] for hh in range(2)], axis=1)
            e = jnp.dot(ut.astype(BF16), bin_ref[d], preferred_element_type=F32)
            a_re, a_im = a_ref[2 * d:2 * d + 1, :], a_ref[2 * d + 1:2 * d + 2, :]
            h_re, h_im = h_ref[2 * d], h_ref[2 * d + 1]
            h_ref[2 * d] = a_re * h_re - a_im * h_im + e[:, 0:gn]
            h_ref[2 * d + 1] = a_re * h_im + a_im * h_re + e[:, gn:2 * gn]

    h_ref[...] = jnp.zeros_like(h_ref)

    def local_step(i, carry):
        advance(i)
        return carry

    lax.fori_loop(0, q, local_step, 0)

    sub = _iota((HALO, gn), 0)

    def boundary(i, carry):
        starts = (pl.multiple_of(i * HALO, HALO), pl.multiple_of(nc - HALO - i * HALO, HALO))
        new = []
        for d in range(2):
            h_re, h_im = carry[2 * d], carry[2 * d + 1]
            a_re, a_im = a_ref[4 + 2 * d:5 + 2 * d, :], a_ref[5 + 2 * d:6 + 2 * d, :]
            s_re = h_ref[2 * d, pl.ds(starts[d], HALO), :]
            s_im = h_ref[2 * d + 1, pl.ds(starts[d], HALO), :]
            out_re = jnp.zeros((HALO, gn), F32)
            out_im = jnp.zeros((HALO, gn), F32)
            for step in range(HALO):
                rix = HALO - 1 - step if d else step
                out_re = jnp.where(sub == rix, h_re, out_re)
                out_im = jnp.where(sub == rix, h_im, out_im)
                n_re = a_re * h_re - a_im * h_im + s_re[rix:rix + 1, :]
                n_im = a_re * h_im + a_im * h_re + s_im[rix:rix + 1, :]
                h_re, h_im = n_re, n_im
            bnd_ref[2 * d, pl.ds(starts[d], HALO), :] = out_re
            bnd_ref[2 * d + 1, pl.ds(starts[d], HALO), :] = out_im
            new += [h_re, h_im]
        return tuple(new)

    zero = jnp.zeros((1, gn), F32)
    lax.fori_loop(0, nc // HALO, boundary, (zero,) * 4)

    h_ref[...] = bnd_ref[...]

    def emit(i, first):
        advance(i)
        for d, t in ((0, i), (1, q - 1 - i)):
            hcat = jnp.concatenate([h_ref[2 * d], h_ref[2 * d + 1]], axis=1).astype(BF16)
            part = jnp.dot(hcat, cout_ref[...], preferred_element_type=F32)
            for hh in range(2):
                rows = token_rows(ys_ref, hh, t)
                piece = part[:, hh * half:(hh + 1) * half]
                rows[...] = piece if first else rows[...] + piece

    def emit_first(i, carry):
        emit(i, True)
        return carry

    def emit_second(i, carry):
        emit(i, False)
        return carry

    lax.fori_loop(0, q // 2, emit_first, 0)
    lax.fori_loop(q // 2, q, emit_second, 0)
    for hh in range(2):
        y_ref[0, :, hh * half:(hh + 1) * half] = ys_ref[hh]


def _s5_mixer_core(u, tables):
    b_in, c_out, a_rows = tables
    bsz, lp, width = u.shape
    nc = lp // S5_Q
    gn = S5_GROUPS * S5_STATE
    row = pl.BlockSpec((1, lp, width), lambda b: (b, 0, 0))
    return pl.pallas_call(
        _s5_kernel, grid=(bsz,),
        in_specs=[row, _const_spec(b_in.shape), _const_spec(c_out.shape), _const_spec(a_rows.shape)],
        out_specs=row, out_shape=jax.ShapeDtypeStruct((bsz, lp, width), F32),
        scratch_shapes=[pltpu.VMEM((4, nc, gn), F32)] * 2 + [pltpu.VMEM((2, lp, width // 2), F32)] * 2,
        compiler_params=_params("parallel"), name="s5_scan")(u, b_in, c_out, a_rows)


def _ssd_pre_kernel(cur_ref, prev_ref, next_ref, cw_ref, cb_ref, xc_ref, *, front):
    j = pl.program_id(1)
    nb = pl.num_programs(1)
    tb = cur_ref.shape[1]
    ext = jnp.concatenate([prev_ref[0] * jnp.where(j > 0, 1.0, 0.0), cur_ref[0],
                           next_ref[0] * jnp.where(j < nb - 1, 1.0, 0.0)], axis=0)
    n = tb + 2 * HALO
    lo = pltpu.roll(pltpu.roll(cw_ref[0:1, :] * ext, 1, 0) + cw_ref[1:2, :] * ext, 1, 0)
    hi = pltpu.roll(pltpu.roll(cw_ref[4:5, :] * ext, n - 1, 0) + cw_ref[3:4, :] * ext, n - 1, 0)
    acc = (lo + hi + cw_ref[2:3, :] * ext)[HALO:HALO + tb] + cb_ref[...]
    valid = (j * tb + _iota((tb, 1), 0)) >= front
    xc_ref[0] = jnp.where(valid, acc * jax.nn.sigmoid(acc), 0.0).astype(BF16)


def _ssd_pre(xbc, conv_w, conv_b, *, front, tb):
    bsz, lp, ch = xbc.shape
    per = tb // HALO
    nh = lp // HALO
    cur = pl.BlockSpec((1, tb, ch), lambda b, j: (b, j, 0))
    prv = pl.BlockSpec((1, HALO, ch), lambda b, j: (b, jnp.maximum(j * per - 1, 0), 0))
    nxt = pl.BlockSpec((1, HALO, ch), lambda b, j: (b, jnp.minimum(j * per + per, nh - 1), 0))
    cw = jnp.pad(conv_w, ((0, HALO - SSD_CONV), (0, 0)))
    return pl.pallas_call(
        functools.partial(_ssd_pre_kernel, front=front), grid=(bsz, lp // tb),
        in_specs=[cur, prv, nxt, _const_spec((HALO, ch)), _const_spec((1, ch))],
        out_specs=cur, out_shape=jax.ShapeDtypeStruct((bsz, lp, ch), BF16),
        compiler_params=_params("parallel", "parallel"), name="ssd_pre",
    )(xbc, xbc, xbc, cw, conv_b.reshape(1, ch))


def _ssd_kernel(xcf_ref, dcf_ref, xcb_ref, dcb_ref, pc_ref, dskip_ref, yf_ref, yb_ref, hf_ref, hb_ref, *, front):
    c = pl.program_id(1)
    nc = pl.num_programs(1)
    q = SSD_CHUNK
    gw = SSD_HPG * SSD_HEAD_DIM

    @pl.when(c == 0)
    def _():
        hf_ref[...] = jnp.zeros_like(hf_ref)
        hb_ref[...] = jnp.zeros_like(hb_ref)

    rr, cc = _iota((q, q), 0), _iota((q, q), 1)
    rowi = _iota((q, 1), 0)
    lane_head = jnp.right_shift(_iota((q, gw), 1), 6)
    spread = jnp.right_shift(_iota((DT_PAD, SSD_WIDTH), 1), 6) == _iota((DT_PAD, SSD_WIDTH), 0)
    dirs = ((xcf_ref, dcf_ref, yf_ref, hf_ref, False, c), (xcb_ref, dcb_ref, yb_ref, hb_ref, True, nc - 1 - c))

    pre = []
    for d, (xc_ref, dc_ref, _, _, rev, cm) in enumerate(dirs):
        valid = (cm * q + rowi) >= front
        tri = (cc >= rr) if rev else (cc <= rr)
        dt = jnp.where(valid, jax.nn.softplus(dc_ref[0] + pc_ref[2 * d:2 * d + 1, :]), 0.0)
        cs = _mm_cumsum(tri, dt * pc_ref[2 * d + 1:2 * d + 2, :])
        tot = cs[0:1, :] if rev else cs[q - 1:q, :]
        wide = _mm_split(jnp.concatenate([dt, jnp.exp(cs), jnp.exp(tot - cs)], axis=0), spread, parts=3)
        dt_e, e_cs, to_end = wide[0:q], wide[q:2 * q], wide[2 * q:3 * q]
        decay = e_cs[0:1, :] if rev else e_cs[q - 1:q, :]
        xf = xc_ref[0, :, 0:SSD_WIDTH].astype(F32)
        xdt = xf * dt_e
        pre.append((tri, cs, cs.T, e_cs, xdt * to_end, decay, xf, xdt))

    for g in range(SSD_GROUPS):
        lanes = slice(g * gw, (g + 1) * gw)
        cbs = []
        for (xc_ref, *_), p in zip(dirs, pre):
            bg = xc_ref[0, :, SSD_WIDTH + g * SSD_STATE:SSD_WIDTH + (g + 1) * SSD_STATE]
            cg = xc_ref[0, :, SSD_WIDTH + (SSD_GROUPS + g) * SSD_STATE:SSD_WIDTH + (SSD_GROUPS + g + 1) * SSD_STATE]
            cbs.append((bg, cg, lax.dot_general(cg, bg, (((1,), (1,)), ((), ())), preferred_element_type=F32)))
        y_diag = []
        for (bg, cg, cb), (tri, cs_c, cs_t, e_cs, xdt_end, decay, xf, xdt) in zip(cbs, pre):
            mcat, xbd = [], []
            for jj in range(SSD_HPG):
                j = g * SSD_HPG + jj
                seg = jnp.exp(jnp.where(tri, cs_c[:, j:j + 1] - cs_t[j:j + 1, :], -jnp.inf))
                mcat.append((cb * seg).astype(BF16))
                xbd.append(jnp.where(lane_head == jj, xdt[:, lanes], 0.0).astype(BF16))
            y_diag.append(jnp.dot(jnp.concatenate(mcat, axis=1), jnp.concatenate(xbd, axis=0),
                                  preferred_element_type=F32))
        for (_, _, y_ref, h_ref, rev, _), (bg, cg, cb), p, yd in zip(dirs, cbs, pre, y_diag):
            tri, cs_c, cs_t, e_cs, xdt_end, decay, xf, xdt = p
            h_in = h_ref[g]
            y = yd + jnp.dot(cg, h_in.astype(BF16), preferred_element_type=F32) * e_cs[:, lanes]
            st = jnp.dot(bg.astype(F32).T.astype(BF16), xdt_end[:, lanes].astype(BF16), preferred_element_type=F32)
            h_ref[g] = h_in * decay[:, lanes] + st
            if not rev:
                y = y + dskip_ref[:, lanes] * xf[:, lanes]
            y_ref[0, :, lanes] = y


def _ssd_scan(xc, dt_c, dt_bias, a_log, d_skip, *, front):
    bsz, lp, ch = xc.shape
    nc = lp // SSD_CHUNK

    def spec(width, rev):
        return pl.BlockSpec((1, SSD_CHUNK, width), (lambda b, c: (b, nc - 1 - c, 0)) if rev else (lambda b, c: (b, c, 0)))

    aneg = -jnp.exp(a_log)
    pc = jnp.stack([jnp.pad(r, (0, DT_PAD - SSD_HEADS)) for r in (dt_bias[0], aneg[0], dt_bias[1], aneg[1])], axis=0)
    dsk = jnp.repeat(d_skip, SSD_HEAD_DIM).reshape(1, SSD_WIDTH)
    gw = SSD_HPG * SSD_HEAD_DIM
    return pl.pallas_call(
        functools.partial(_ssd_kernel, front=front), grid=(bsz, nc),
        in_specs=[spec(ch, False), spec(DT_PAD, False), spec(ch, True), spec(DT_PAD, True),
                  _const_spec(pc.shape), _const_spec(dsk.shape)],
        out_specs=[spec(SSD_WIDTH, False), spec(SSD_WIDTH, True)],
        out_shape=[jax.ShapeDtypeStruct((bsz, lp, SSD_WIDTH), F32)] * 2,
        scratch_shapes=[pltpu.VMEM((SSD_GROUPS, SSD_STATE, gw), F32)] * 2,
        compiler_params=_params("parallel", "arbitrary"), name="ssd_scan",
    )(xc, dt_c, xc, dt_c, pc, dsk)


def _head_ones():
    rr, cc = _iota((RWKV_WIDTH, RWKV_WIDTH), 0), _iota((RWKV_WIDTH, RWKV_WIDTH), 1)
    return (jnp.right_shift(rr, 6) == jnp.right_shift(cc, 6)).astype(F32)


def _rwkv_pre_kernel(cur_ref, prev_ref, next_ref, mu_ref, vec_ref, g1_ref, g2_ref, w1_ref, w2_ref, a1_ref,
                     a2_ref, w0a0_ref,
                     r_ref, v_ref, nkk_ref, g_ref, bonus_ref, lw0_ref, kd0_ref, b0_ref, lw1_ref, kd1_ref,
                     b1_ref, *, front):
    j = pl.program_id(1)
    nb = pl.num_programs(1)
    tb = cur_ref.shape[1]
    w = RWKV_WIDTH
    cur = cur_ref[0]
    prow = prev_ref[0, HALO - 1:HALO, :] * jnp.where(j > 0, 1.0, 0.0)
    nrow = next_ref[0, 0:1, :] * jnp.where(j < nb - 1, 1.0, 0.0)
    rowi = _iota((tb, 1), 0)
    prev = jnp.where(rowi == 0, prow, pltpu.roll(cur, 1, 0))
    nxt = jnp.where(rowi == tb - 1, nrow, pltpu.roll(cur, tb - 1, 0))
    shift = 0.5 * (prev + nxt) - cur
    valid = (j * tb + rowi) >= front
    r = cur[:, 0:w] + shift[:, 0:w] * mu_ref[0:1, :]
    k = jnp.where(valid, cur[:, w:2 * w] + shift[:, w:2 * w] * mu_ref[1:2, :], 0.0)
    v = jnp.where(valid, cur[:, 2 * w:3 * w] + shift[:, 2 * w:3 * w] * mu_ref[2:3, :], 0.0)
    xc, dxc = cur[:, 3 * w:], shift[:, 3 * w:]
    xw = xc + dxc * mu_ref[3:4, :]
    xa = xc + dxc * mu_ref[4:5, :]
    xg = xc + dxc * mu_ref[5:6, :]
    k_k, k_a, r_k = vec_ref[0:1, :], vec_ref[1:2, :], vec_ref[2:3, :]
    ones = _head_ones()
    g_ref[0] = _mm(jax.nn.sigmoid(_mm(xg, g1_ref[...])), g2_ref[...])
    kk = k * k_k
    kk = kk * lax.rsqrt(_mm_split(kk * kk, ones) + 1e-12)
    r_ref[0] = r
    v_ref[0] = v
    nkk_ref[0] = -kk
    bonus_ref[0] = _mm_split(r * k * r_k, ones) * v
    for d, (lw_ref, kd_ref, b_ref) in enumerate(((lw0_ref, kd0_ref, b0_ref), (lw1_ref, kd1_ref, b1_ref))):
        lw = w0a0_ref[d:d + 1, :] + _mm(jnp.tanh(_mm(xw, w1_ref[d])), w2_ref[d])
        w_log = -jax.nn.softplus(-lw) - 0.5
        lw_ref[0] = -jnp.exp(w_log)
        ag = jax.nn.sigmoid(w0a0_ref[2 + d:3 + d, :] + _mm(_mm(xa, a1_ref[d]), a2_ref[d]))
        kd_ref[0] = k * (1.0 + (ag - 1.0) * k_a)
        b_ref[0] = kk * ag


def _rwkv_pre(rkvx, mu, vec, g1, g2, w1, w2, a1, a2, w0a0, *, front, tb):
    bsz, lp, _ = rkvx.shape
    per = tb // HALO
    nh = lp // HALO
    w = RWKV_WIDTH
    cur = pl.BlockSpec((1, tb, 4 * w), lambda b, j: (b, j, 0))
    prv = pl.BlockSpec((1, HALO, 4 * w), lambda b, j: (b, jnp.maximum(j * per - 1, 0), 0))
    nxt = pl.BlockSpec((1, HALO, 4 * w), lambda b, j: (b, jnp.minimum(j * per + per, nh - 1), 0))
    out = pl.BlockSpec((1, tb, w), lambda b, j: (b, j, 0))
    consts = [mu, vec, g1, g2, w1, w2, a1, a2, w0a0]
    return pl.pallas_call(
        functools.partial(_rwkv_pre_kernel, front=front), grid=(bsz, lp // tb),
        in_specs=[cur, prv, nxt] + [_const_spec(c.shape) for c in consts],
        out_specs=[out] * 11,
        out_shape=[jax.ShapeDtypeStruct((bsz, lp, w), F32)] * 11,
        compiler_params=_params("parallel", "parallel"),
        name="rwkv_pre",
    )(rkvx, rkvx, rkvx, *consts)


def _rwkv_chunk_ops(chains):
    n = RWKV_CHUNK
    w = RWKV_WIDTH
    lane_head = jnp.right_shift(_iota((n, w), 1), 6)

    def stack(x):
        return jnp.concatenate([jnp.where(lane_head == h, x, 0.0) for h in range(RWKV_HEADS)], axis=0)

    rr, cc = _iota((w, w), 0), _iota((w, w), 1)
    same = jnp.right_shift(rr, 6) == jnp.right_shift(cc, 6)
    tl, sl = jnp.bitwise_and(rr, n - 1), jnp.bitwise_and(cc, n - 1)
    eye = (rr == cc).astype(F32)
    r8, c8 = _iota((n, n), 0), _iota((n, n), 1)
    masks = {}
    for rev in sorted({ch[6] for ch in chains}):
        masks[rev] = (same & ((sl > tl) if rev else (sl < tl)), same & ((sl >= tl) if rev else (sl <= tl)),
                      (c8 >= r8) if rev else (c8 <= r8))

    pre = []
    for r, v, a, kd, b, lw, rev in chains:
        cum = _mm_cumsum(masks[rev][2], lw)
        tot = cum[0:1, :] if rev else cum[n - 1:n, :]
        p_inv = jnp.exp(-cum)
        p_end = jnp.exp(tot - cum)
        r_t = stack(r * jnp.exp(cum))
        a_t = stack(a * jnp.exp(cum - lw))
        lhs = jnp.concatenate([a_t, r_t], axis=0)
        rhs = jnp.concatenate([stack(b * p_inv), stack(kd * p_inv)], axis=0)
        pre.append((lhs, rhs, r_t, a_t, stack(b * p_end).T, stack(kd * p_end).T, stack(v), jnp.exp(tot)))

    grams = [_mm_nt(p[0], p[1]) for p in pre]
    parts = []
    for ch, gram in zip(chains, grams):
        strict, incl, _ = masks[ch[6]]
        parts.append((jnp.where(strict, gram[0:w, 0:w], 0.0), jnp.where(strict, gram[0:w, w:2 * w], 0.0),
                      jnp.where(incl, gram[w:2 * w, 0:w], 0.0), jnp.where(incl, gram[w:2 * w, w:2 * w], 0.0)))

    def same_block(shift):
        return jnp.right_shift(rr, shift) == jnp.right_shift(cc, shift)

    invs = [eye + jnp.where(same_block(1), p[0], 0.0) for p in parts]
    on_v = [_mm(jnp.concatenate([p[1], p[3], q[5]], axis=0), q[6]) for p, q in zip(parts, pre)]
    for shift in range(1, RWKV_CHUNK.bit_length() - 1):
        between = same_block(shift + 1) & (jnp.right_shift(rr, shift) != jnp.right_shift(cc, shift))
        ys = [_mm(jnp.where(between, p[0], 0.0), inv) for p, inv in zip(parts, invs)]
        invs = [inv + _mm(inv, y) for inv, y in zip(invs, ys)]
    wus = [_mm(inv, jnp.concatenate([q[3], ov[0:w]], axis=1))
           for inv, q, ov in zip(invs, pre, on_v)]
    on_wu = [_mm(jnp.concatenate([p[2], q[4]], axis=0), wu) for p, q, wu in zip(parts, pre, wus)]
    out = []
    for q, ov, ow in zip(pre, on_v, on_wu):
        rp = q[2] + ow[0:w, 0:w]
        y0 = ow[0:w, w:2 * w] + ov[w:2 * w]
        mk = eye * q[7] + ow[w:2 * w, 0:w]
        nk = ow[w:2 * w, w:2 * w] + ov[2 * w:3 * w]
        out.append((y0, rp, mk, nk))
    return out


def _rwkv_scan_kernel(rf_ref, vf_ref, af_ref, lwf_ref, kdf_ref, bf_ref,
                      rb_ref, vb_ref, ab_ref, lwb_ref, kdb_ref, bb_ref,
                      yf_ref, yb_ref, sf_ref, sb_ref, *, group):
    n = RWKV_CHUNK

    @pl.when(pl.program_id(1) == 0)
    def _():
        sf_ref[...] = jnp.zeros_like(sf_ref)
        sb_ref[...] = jnp.zeros_like(sb_ref)

    def unstack(x):
        out = x[0:n]
        for h in range(1, RWKV_HEADS):
            out = out + x[h * n:(h + 1) * n]
        return out

    dirs = (((rf_ref, vf_ref, af_ref, kdf_ref, bf_ref, lwf_ref), yf_ref, sf_ref, False),
            ((rb_ref, vb_ref, ab_ref, kdb_ref, bb_ref, lwb_ref), yb_ref, sb_ref, True))
    chains = []
    for step in range(group):
        for refs, _, _, reverse in dirs:
            gi = group - 1 - step if reverse else step
            chains.append(tuple(ref[0, gi * n:(gi + 1) * n, :] for ref in refs) + (reverse,))
    ops = _rwkv_chunk_ops(chains)
    states = [s_ref[...] for _, _, s_ref, _ in dirs]
    for step in range(group):
        for d, (_, y_ref, _, reverse) in enumerate(dirs):
            gi = group - 1 - step if reverse else step
            y0, rp, mk, nk = ops[step * len(dirs) + d]
            on_st = _mm(jnp.concatenate([rp, mk], axis=0), states[d])
            y_ref[0, gi * n:(gi + 1) * n, :] = unstack(y0 + on_st[0:RWKV_WIDTH])
            states[d] = on_st[RWKV_WIDTH:] + nk
    for d, (_, _, s_ref, _) in enumerate(dirs):
        s_ref[...] = states[d]


def _rwkv_scan(r, v, nkk, lw0, kd0, b0, lw1, kd1, b1, *, group):
    bsz, lp, w = r.shape
    rows = group * RWKV_CHUNK
    nb = lp // rows
    fwd = pl.BlockSpec((1, rows, w), lambda bb, c: (bb, c, 0))
    bwd = pl.BlockSpec((1, rows, w), lambda bb, c: (bb, nb - 1 - c, 0))
    return pl.pallas_call(
        functools.partial(_rwkv_scan_kernel, group=group), grid=(bsz, nb),
        in_specs=[fwd] * 6 + [bwd] * 6, out_specs=[fwd, bwd],
        out_shape=[jax.ShapeDtypeStruct((bsz, lp, w), F32)] * 2,
        scratch_shapes=[pltpu.VMEM((w, w), F32)] * 2,
        compiler_params=_params("parallel", "arbitrary"),
        name="rwkv_scan",
    )(r, v, nkk, lw0, kd0, b0, r, v, nkk, lw1, kd1, b1)


def _merge_kernel(h_ref, gates_ref, ys5_ref, ua_ref, ysf_ref, ysb_ref, z_ref, ycf_ref, ycb_ref, g_ref,
                  bonus_ref, vec_a_ref, glu_w_ref, glu_b_ref, ssd_nw_ref, ln_ref, pa_ref, pb_ref, pc_ref,
                  wo_ref, o_ref):
    ua = ua_ref[...]
    ya = jax.nn.gelu(ys5_ref[...] + vec_a_ref[...] * ua)
    zz = _mm(ya, glu_w_ref[...]) + glu_b_ref[...]
    ya = zz[:, :S5_WIDTH] * jax.nn.sigmoid(zz[:, S5_WIDTH:])
    z = z_ref[...].astype(F32)
    yb = (ysf_ref[...] + ysb_ref[...]) * (z * jax.nn.sigmoid(z))
    yb = yb * lax.rsqrt(jnp.mean(yb * yb, -1, keepdims=True) + EPS) * ssd_nw_ref[...]
    yc = ycf_ref[...] + ycb_ref[...]
    avg = _head_ones() * (1.0 / RWKV_HEAD)
    mean = _mm_split(yc, avg)
    dev = yc - mean
    var = _mm_split(dev * dev, avg)
    yc = dev * lax.rsqrt(var + RWKV_LN_EPS) * ln_ref[0:1, :] + ln_ref[1:2, :]
    yc = (yc + bonus_ref[...]) * g_ref[...]
    d = D_MODEL
    merged = (jax.nn.sigmoid(gates_ref[:, 0:d].astype(F32)) * _mm(ya, pa_ref[...])
              + jax.nn.sigmoid(gates_ref[:, d:2 * d].astype(F32)) * _mm(yb, pb_ref[...])
              + jax.nn.sigmoid(gates_ref[:, 2 * d:3 * d].astype(F32)) * _mm(yc, pc_ref[...]))
    o_ref[...] = h_ref[...] + _mm(merged, wo_ref[...])


def _merge(h2d, gates, ys5, ua, ysf, ysb, z, ycf, ycb, g, bonus, consts, tm):
    t = h2d.shape[0]
    toks = [h2d, gates, ys5, ua, ysf, ysb, z, ycf, ycb, g, bonus]
    return pl.pallas_call(
        _merge_kernel, grid=(t // tm,),
        in_specs=[pl.BlockSpec((tm, a.shape[1]), lambda i: (i, 0)) for a in toks]
        + [_const_spec(c.shape) for c in consts],
        out_specs=pl.BlockSpec((tm, D_MODEL), lambda i: (i, 0)),
        out_shape=jax.ShapeDtypeStruct((t, D_MODEL), F32),
        compiler_params=_params("parallel"),
        name="merge",
    )(*toks, *consts)


def _mlp_kernel(h_ref, nw_ref, w1_ref, w2_ref, fw_ref, o_ref, *, front, rows_per_batch, final):
    x = h_ref[...]
    tm = x.shape[0]
    xn = x * lax.rsqrt(jnp.mean(x * x, -1, keepdims=True) + EPS) * nw_ref[...]
    a = jnp.maximum(jnp.dot(xn.astype(BF16), w1_ref[...], preferred_element_type=F32), 0.0)
    y = x + jnp.dot((a * a).astype(BF16), w2_ref[...], preferred_element_type=F32)
    if final:
        y = y * lax.rsqrt(jnp.mean(y * y, -1, keepdims=True) + EPS) * fw_ref[...]
    row = (pl.program_id(0) * tm) % rows_per_batch + _iota((tm, 1), 0)
    row = jnp.where(row >= rows_per_batch, row - rows_per_batch, row)
    o_ref[...] = jnp.where(row >= front, y, 0.0)


def _mlp(h2d, norm_w, w1, w2, final_w, *, front, rows_per_batch, final, tm):
    t = h2d.shape[0]
    assert tm <= rows_per_batch
    kern = functools.partial(_mlp_kernel, front=front, rows_per_batch=rows_per_batch, final=final)
    return pl.pallas_call(
        kern, grid=(t // tm,),
        in_specs=[pl.BlockSpec((tm, D_MODEL), lambda i: (i, 0)), _const_spec((1, D_MODEL)),
                  pl.BlockSpec((D_MODEL, D_FF), lambda i: (0, 0), pipeline_mode=pl.Buffered(1)),
                  pl.BlockSpec((D_FF, D_MODEL), lambda i: (0, 0), pipeline_mode=pl.Buffered(1)),
                  _const_spec((1, D_MODEL))],
        out_specs=pl.BlockSpec((tm, D_MODEL), lambda i: (i, 0)),
        out_shape=jax.ShapeDtypeStruct((t, D_MODEL), F32),
        compiler_params=_params("parallel"),
        name="mlp",
    )(h2d, norm_w.reshape(1, D_MODEL), w1.astype(BF16), w2.astype(BF16), final_w.reshape(1, D_MODEL))


def _token_tile(t, cap):
    tm = cap
    while t % tm:
        tm //= 2
    return tm


def _row_tile(lp, cap):
    k = lp // ROW_ALIGN
    best = 1
    for m in range(1, k + 1):
        if k % m == 0 and m * ROW_ALIGN <= cap:
            best = m
    return best * ROW_ALIGN


def kernel(x, meta_tokens, final_norm_w, mix_norm_w, w_in, s5_lambda_re, s5_lambda_im, s5_log_step, s5_b_re, s5_b_im, s5_c_re, s5_c_im, s5_d, s5_glu_w, s5_glu_b, ssd_conv_w, ssd_conv_b, ssd_a_log, ssd_dt_bias, ssd_d, ssd_norm_w, rwkv_mu_rkv, rwkv_mu_wag, rwkv_w0, rwkv_w1, rwkv_w2, rwkv_a0, rwkv_a1, rwkv_a2, rwkv_g1, rwkv_g2, rwkv_k_k, rwkv_k_a, rwkv_r_k, rwkv_ln_w, rwkv_ln_b, proj_a, proj_b, proj_c, w_out, mlp_norm_w, mlp_w1, mlp_w2):
    bsz, seq, d = x.shape
    assert d == D_MODEL
    length = N_META + seq
    lp = -(-length // ROW_ALIGN) * ROW_ALIGN
    front = lp - length
    t = bsz * lp
    depth = w_in.shape[0]
    meta = jnp.broadcast_to(meta_tokens[None].astype(x.dtype), (bsz, N_META, d))
    h = jnp.concatenate([jnp.zeros((bsz, front, d), x.dtype), meta, x], axis=1).reshape(t, d)
    tm = _token_tile(t, 512)
    tb = _row_tile(lp, 512)
    rwkv_group = 2

    for i in range(depth):
        xbc, rkvx, gates, z, ua, dt_c = _in_proj(h, mix_norm_w[i], _permute_w_in(w_in[i]), tm)
        tables = _s5_tables(s5_lambda_re[i], s5_lambda_im[i], s5_log_step[i], s5_b_re[i], s5_b_im[i],
                            s5_c_re[i], s5_c_im[i])
        ys5 = _s5_mixer_core(ua.reshape(bsz, lp, S5_WIDTH), tables).reshape(t, S5_WIDTH)
        xc = _ssd_pre(xbc.reshape(bsz, lp, SSD_CONV_CH), ssd_conv_w[i], ssd_conv_b[i], front=front, tb=tb)
        ys = _ssd_scan(xc, dt_c.reshape(bsz, lp, DT_PAD), ssd_dt_bias[i], ssd_a_log[i], ssd_d[i], front=front)
        ys = [y.reshape(t, SSD_WIDTH) for y in ys]
        mu = jnp.concatenate([rwkv_mu_rkv[i], rwkv_mu_wag[i]], axis=0)
        vec = jnp.stack([rwkv_k_k[i], rwkv_k_a[i], rwkv_r_k[i].reshape(RWKV_WIDTH)], axis=0)
        w0a0 = jnp.concatenate([rwkv_w0[i], rwkv_a0[i]], axis=0)
        pre = _rwkv_pre(rkvx.reshape(bsz, lp, 4 * RWKV_WIDTH), mu, vec, rwkv_g1[i], rwkv_g2[i], rwkv_w1[i],
                        rwkv_w2[i], rwkv_a1[i], rwkv_a2[i], w0a0, front=front, tb=tb)
        r, v, nkk, g, bonus, lw0, kd0, b0, lw1, kd1, b1 = pre
        ycf, ycb = _rwkv_scan(r, v, nkk, lw0, kd0, b0, lw1, kd1, b1, group=rwkv_group)
        ycf, ycb = ycf.reshape(t, RWKV_WIDTH), ycb.reshape(t, RWKV_WIDTH)
        consts = [s5_d[i].reshape(1, S5_WIDTH), s5_glu_w[i].astype(BF16), s5_glu_b[i].reshape(1, 2 * S5_WIDTH),
                  ssd_norm_w[i].reshape(1, SSD_WIDTH), jnp.stack([rwkv_ln_w[i], rwkv_ln_b[i]], axis=0),
                  proj_a[i].astype(BF16), proj_b[i].astype(BF16), proj_c[i].astype(BF16),
                  w_out[i].astype(BF16)]
        h = _merge(h, gates, ys5, ua, ys[0], ys[1], z, ycf, ycb, g.reshape(t, RWKV_WIDTH),
                   bonus.reshape(t, RWKV_WIDTH), consts, tm)
        h = _mlp(h, mlp_norm_w[i], mlp_w1[i], mlp_w2[i], final_norm_w, front=front, rows_per_batch=lp,
                 final=(i == depth - 1), tm=tm)
    return h.reshape(bsz, lp, d)[:, front + N_META:]
```

```python
import functools

import jax
import jax.numpy as jnp
from jax import lax
from jax.experimental import pallas as pl
from jax.experimental.pallas import tpu as pltpu

D_MODEL = 1024
N_META = 16
EPS = 1e-6
D_FF = 4 * D_MODEL

S5_WIDTH = 256
S5_GROUP = 16
S5_GROUPS = 16
S5_STATE = 64
S5_Q = 16

SSD_WIDTH = 512
SSD_HEAD_DIM = 64
SSD_HEADS = 8
SSD_GROUPS = 2
SSD_HPG = SSD_HEADS // SSD_GROUPS
SSD_STATE = 128
SSD_CONV = 5
SSD_CHUNK = 128
SSD_CONV_CH = SSD_WIDTH + 2 * SSD_GROUPS * SSD_STATE

RWKV_WIDTH = 256
RWKV_HEAD = 64
RWKV_HEADS = 4
RWKV_LN_EPS = 64e-5
RWKV_CHUNK = 64

N_BRANCH = 3
DT_PAD = 128
ROW_ALIGN = 128
HALO = 8

F32 = jnp.float32
BF16 = jnp.bfloat16
VMEM_LIMIT = 56 * 1024 * 1024


def _mm(a, b):
    return jnp.dot(a.astype(BF16), b.astype(BF16), preferred_element_type=F32)


def _bf16_parts(a, parts):
    out = []
    rest = a
    for _ in range(parts):
        part = rest.astype(BF16)
        rest = rest - part.astype(F32)
        out.append(part)
    return out


def _mm_split(a, b, parts=2):
    bb = b.astype(BF16)
    terms = [jnp.dot(p, bb, preferred_element_type=F32) for p in _bf16_parts(a, parts)]
    return functools.reduce(lambda x, y: x + y, terms)


def _mm_cumsum(tri, a):
    t = tri.astype(BF16)
    terms = [jnp.dot(t, p, preferred_element_type=F32) for p in _bf16_parts(a, 3)]
    return functools.reduce(lambda x, y: x + y, terms)


def _mm_nt(a, b):
    return lax.dot_general(a.astype(BF16), b.astype(BF16), (((1,), (1,)), ((), ())),
                           preferred_element_type=F32)


def _iota(shape, dim):
    return lax.broadcasted_iota(jnp.int32, shape, dim)


def _params(*sem):
    return pltpu.CompilerParams(dimension_semantics=sem, vmem_limit_bytes=VMEM_LIMIT)


def _const_spec(shape):
    nd = len(shape)
    return pl.BlockSpec(shape, lambda *_: (0,) * nd)


def _in_proj_kernel(h_ref, nw_ref, w_ref, xbc_ref, rkvx_ref, gates_ref, z_ref, ua_ref, dt_ref):
    x = h_ref[...]
    xn = x * lax.rsqrt(jnp.mean(x * x, -1, keepdims=True) + EPS) * nw_ref[...]
    xb = xn.astype(BF16)
    off = 0
    for ref in (xbc_ref, rkvx_ref, gates_ref, z_ref, ua_ref, dt_ref):
        n = ref.shape[-1]
        ref[...] = jnp.dot(xb, w_ref[:, off:off + n], preferred_element_type=F32).astype(ref.dtype)
        off += n


def _in_proj(h2d, norm_w, w_perm, tm):
    t = h2d.shape[0]
    widths = (SSD_CONV_CH, 4 * RWKV_WIDTH, N_BRANCH * D_MODEL, SSD_WIDTH, S5_WIDTH, DT_PAD)
    dtypes = (F32, F32, BF16, BF16, F32, F32)
    n_all = sum(widths)
    return pl.pallas_call(
        _in_proj_kernel,
        grid=(t // tm,),
        in_specs=[pl.BlockSpec((tm, D_MODEL), lambda i: (i, 0)),
                  _const_spec((1, D_MODEL)),
                  pl.BlockSpec((D_MODEL, n_all), lambda i: (0, 0), pipeline_mode=pl.Buffered(1))],
        out_specs=[pl.BlockSpec((tm, n), lambda i: (i, 0)) for n in widths],
        out_shape=[jax.ShapeDtypeStruct((t, n), dt) for n, dt in zip(widths, dtypes)],
        compiler_params=_params("parallel"),
        name="in_proj",
    )(h2d, norm_w.reshape(1, D_MODEL), w_perm)


def _permute_w_in(w):
    o_u, o_z, o_xbc = 0, S5_WIDTH, S5_WIDTH + SSD_WIDTH
    o_dt = o_xbc + SSD_CONV_CH
    o_rkvx = o_dt + SSD_HEADS
    o_g = o_rkvx + 4 * RWKV_WIDTH
    dt_cols = jnp.pad(w[:, o_dt:o_rkvx], ((0, 0), (0, DT_PAD - SSD_HEADS)))
    return jnp.concatenate([w[:, o_xbc:o_dt], w[:, o_rkvx:o_g], w[:, o_g:], w[:, o_z:o_xbc],
                            w[:, o_u:o_z], dt_cols], axis=1).astype(BF16)


def _s5_tables(lam_re, lam_im, log_step, b_re, b_im, c_re, c_im):
    g, n, hh = S5_GROUPS, S5_STATE, S5_GROUP
    step = jnp.exp(log_step)[:, :, None]

    def power(j):
        mag = jnp.exp(lam_re * step * j)
        return mag * jnp.cos(lam_im * step * j), mag * jnp.sin(lam_im * step * j)

    ab_re, ab_im = power(1.0)
    aq_re, aq_im = power(float(S5_Q))
    den = lam_re * lam_re + lam_im * lam_im
    co_re = ((ab_re - 1.0) * lam_re + ab_im * lam_im) / den
    co_im = (ab_im * lam_re - (ab_re - 1.0) * lam_im) / den
    eb_re = co_re[..., None] * b_re - co_im[..., None] * b_im
    eb_im = co_re[..., None] * b_im + co_im[..., None] * b_re
    eye = jnp.eye(g, dtype=F32)
    b_in = jnp.stack([jnp.einsum('dgni,gk->dgikn', e, eye) for e in (eb_re, eb_im)], axis=3)
    b_in = b_in.reshape(2, g * hh, 2 * g * n).astype(BF16)
    c_out = jnp.stack([jnp.einsum('gon,gk->gnko', c, eye) for c in (c_re, -c_im)], axis=0)
    c_out = c_out.reshape(2 * g * n, g * hh).astype(BF16)
    rows = [ab_re[0], ab_im[0], ab_re[1], ab_im[1], aq_re[0], aq_im[0], aq_re[1], aq_im[1]]
    return b_in, c_out, jnp.stack([r.reshape(g * n) for r in rows], axis=0)


def _s5_kernel(u_ref, bin_ref, cout_ref, a_ref, y_ref, h_ref, bnd_ref, us_ref, ys_ref):
    q = S5_Q
    nc = u_ref.shape[1] // q
    gn = S5_GROUPS * S5_STATE
    half = S5_WIDTH // 2
    for hh in range(2):
        us_ref[hh] = u_ref[0, :, hh * half:(hh + 1) * half]

    def token_rows(ref, hh, t):
        return ref.at[hh, pl.ds(t, nc, stride=q), :]

    def advance(i):
        for d, t in ((0, i), (1, q - 1 - i)):
            ut = jnp.concatenate([token_rows(us_ref, hh, t)[...] for hh in range(2)], axis=1)
            e = jnp.dot(ut.astype(BF16), bin_ref[d], preferred_element_type=F32)
            a_re, a_im = a_ref[2 * d:2 * d + 1, :], a_ref[2 * d + 1:2 * d + 2, :]
            h_re, h_im = h_ref[2 * d], h_ref[2 * d + 1]
            h_ref[2 * d] = a_re * h_re - a_im * h_im + e[:, 0:gn]
            h_ref[2 * d + 1] = a_re * h_im + a_im * h_re + e[:, gn:2 * gn]

    h_ref[...] = jnp.zeros_like(h_ref)

    def local_step(i, carry):
        advance(i)
        return carry

    lax.fori_loop(0, q, local_step, 0)

    sub = _iota((HALO, gn), 0)

    def boundary(i, carry):
        starts = (pl.multiple_of(i * HALO, HALO), pl.multiple_of(nc - HALO - i * HALO, HALO))
        new = []
        for d in range(2):
            h_re, h_im = carry[2 * d], carry[2 * d + 1]
            a_re, a_im = a_ref[4 + 2 * d:5 + 2 * d, :], a_ref[5 + 2 * d:6 + 2 * d, :]
            s_re = h_ref[2 * d, pl.ds(starts[d], HALO), :]
            s_im = h_ref[2 * d + 1, pl.ds(starts[d], HALO), :]
            out_re = jnp.zeros((HALO, gn), F32)
            out_im = jnp.zeros((HALO, gn), F32)
            for step in range(HALO):
                rix = HALO - 1 - step if d else step
                out_re = jnp.where(sub == rix, h_re, out_re)
                out_im = jnp.where(sub == rix, h_im, out_im)
                n_re = a_re * h_re - a_im * h_im + s_re[rix:rix + 1, :]
                n_im = a_re * h_im + a_im * h_re + s_im[rix:rix + 1, :]
                h_re, h_im = n_re, n_im
            bnd_ref[2 * d, pl.ds(starts[d], HALO), :] = out_re
            bnd_ref[2 * d + 1, pl.ds(starts[d], HALO), :] = out_im
            new += [h_re, h_im]
        return tuple(new)

    zero = jnp.zeros((1, gn), F32)
    lax.fori_loop(0, nc // HALO, boundary, (zero,) * 4)

    h_ref[...] = bnd_ref[...]

    def emit(i, first):
        advance(i)
        for d, t in ((0, i), (1, q - 1 - i)):
            hcat = jnp.concatenate([h_ref[2 * d], h_ref[2 * d + 1]], axis=1).astype(BF16)
            part = jnp.dot(hcat, cout_ref[...], preferred_element_type=F32)
            for hh in range(2):
                rows = token_rows(ys_ref, hh, t)
                piece = part[:, hh * half:(hh + 1) * half]
                rows[...] = piece if first else rows[...] + piece

    def emit_first(i, carry):
        emit(i, True)
        return carry

    def emit_second(i, carry):
        emit(i, False)
        return carry

    lax.fori_loop(0, q // 2, emit_first, 0)
    lax.fori_loop(q // 2, q, emit_second, 0)
    for hh in range(2):
        y_ref[0, :, hh * half:(hh + 1) * half] = ys_ref[hh]


def _s5_mixer_core(u, tables):
    b_in, c_out, a_rows = tables
    bsz, lp, width = u.shape
    nc = lp // S5_Q
    gn = S5_GROUPS * S5_STATE
    row = pl.BlockSpec((1, lp, width), lambda b: (b, 0, 0))
    return pl.pallas_call(
        _s5_kernel, grid=(bsz,),
        in_specs=[row, _const_spec(b_in.shape), _const_spec(c_out.shape), _const_spec(a_rows.shape)],
        out_specs=row, out_shape=jax.ShapeDtypeStruct((bsz, lp, width), F32),
        scratch_shapes=[pltpu.VMEM((4, nc, gn), F32)] * 2 + [pltpu.VMEM((2, lp, width // 2), F32)] * 2,
        compiler_params=_params("parallel"), name="s5_scan")(u, b_in, c_out, a_rows)


def _ssd_pre_kernel(cur_ref, prev_ref, next_ref, cw_ref, cb_ref, xc_ref, *, front):
    j = pl.program_id(1)
    nb = pl.num_programs(1)
    tb = cur_ref.shape[1]
    ext = jnp.concatenate([prev_ref[0] * jnp.where(j > 0, 1.0, 0.0), cur_ref[0],
                           next_ref[0] * jnp.where(j < nb - 1, 1.0, 0.0)], axis=0)
    n = tb + 2 * HALO
    lo = pltpu.roll(pltpu.roll(cw_ref[0:1, :] * ext, 1, 0) + cw_ref[1:2, :] * ext, 1, 0)
    hi = pltpu.roll(pltpu.roll(cw_ref[4:5, :] * ext, n - 1, 0) + cw_ref[3:4, :] * ext, n - 1, 0)
    acc = (lo + hi + cw_ref[2:3, :] * ext)[HALO:HALO + tb] + cb_ref[...]
    valid = (j * tb + _iota((tb, 1), 0)) >= front
    xc_ref[0] = jnp.where(valid, acc * jax.nn.sigmoid(acc), 0.0).astype(BF16)


def _ssd_pre(xbc, conv_w, conv_b, *, front, tb):
    bsz, lp, ch = xbc.shape
    per = tb // HALO
    nh = lp // HALO
    cur = pl.BlockSpec((1, tb, ch), lambda b, j: (b, j, 0))
    prv = pl.BlockSpec((1, HALO, ch), lambda b, j: (b, jnp.maximum(j * per - 1, 0), 0))
    nxt = pl.BlockSpec((1, HALO, ch), lambda b, j: (b, jnp.minimum(j * per + per, nh - 1), 0))
    cw = jnp.pad(conv_w, ((0, HALO - SSD_CONV), (0, 0)))
    return pl.pallas_call(
        functools.partial(_ssd_pre_kernel, front=front), grid=(bsz, lp // tb),
        in_specs=[cur, prv, nxt, _const_spec((HALO, ch)), _const_spec((1, ch))],
        out_specs=cur, out_shape=jax.ShapeDtypeStruct((bsz, lp, ch), BF16),
        compiler_params=_params("parallel", "parallel"), name="ssd_pre",
    )(xbc, xbc, xbc, cw, conv_b.reshape(1, ch))


def _ssd_kernel(xcf_ref, dcf_ref, xcb_ref, dcb_ref, pc_ref, dskip_ref, yf_ref, yb_ref, hf_ref, hb_ref, *,
                front, group):
    c = pl.program_id(1)
    nb = pl.num_programs(1)
    q = SSD_CHUNK
    gw = SSD_HPG * SSD_HEAD_DIM

    @pl.when(c == 0)
    def _():
        hf_ref[...] = jnp.zeros_like(hf_ref)
        hb_ref[...] = jnp.zeros_like(hb_ref)

    rr, cc = _iota((q, q), 0), _iota((q, q), 1)
    rowi = _iota((q, 1), 0)
    lane_head = jnp.right_shift(_iota((q, gw), 1), 6)
    spread = jnp.right_shift(_iota((DT_PAD, SSD_WIDTH), 1), 6) == _iota((DT_PAD, SSD_WIDTH), 0)
    dirs = ((xcf_ref, dcf_ref, yf_ref, hf_ref, False, c), (xcb_ref, dcb_ref, yb_ref, hb_ref, True, nb - 1 - c))

    chains = []
    for step in range(group):
        for d, (xc_ref, dc_ref, _, _, rev, blk) in enumerate(dirs):
            gi = group - 1 - step if rev else step
            rows = slice(gi * q, (gi + 1) * q)
            valid = ((blk * group + gi) * q + rowi) >= front
            tri = (cc >= rr) if rev else (cc <= rr)
            dt = jnp.where(valid, jax.nn.softplus(dc_ref[0, rows, :] + pc_ref[2 * d:2 * d + 1, :]), 0.0)
            cs = _mm_cumsum(tri, dt * pc_ref[2 * d + 1:2 * d + 2, :])
            tot = cs[0:1, :] if rev else cs[q - 1:q, :]
            wide = _mm_split(jnp.concatenate([dt, jnp.exp(cs), jnp.exp(tot - cs)], axis=0), spread, parts=3)
            dt_e, e_cs, to_end = wide[0:q], wide[q:2 * q], wide[2 * q:3 * q]
            decay = e_cs[0:1, :] if rev else e_cs[q - 1:q, :]
            xf = xc_ref[0, rows, 0:SSD_WIDTH].astype(F32)
            xdt = xf * dt_e
            chains.append((d, rows, tri, cs, cs.T, e_cs, xdt * to_end, decay, xf, xdt))

    for g in range(SSD_GROUPS):
        lanes = slice(g * gw, (g + 1) * gw)
        b0 = SSD_WIDTH + g * SSD_STATE
        c0 = SSD_WIDTH + (SSD_GROUPS + g) * SSD_STATE
        local = []
        for d, rows, tri, cs, cs_t, e_cs, xdt_end, decay, xf, xdt in chains:
            xc_ref = dirs[d][0]
            bg = xc_ref[0, rows, b0:b0 + SSD_STATE]
            cg = xc_ref[0, rows, c0:c0 + SSD_STATE]
            cb = lax.dot_general(cg, bg, (((1,), (1,)), ((), ())), preferred_element_type=F32)
            mcat, xbd = [], []
            for jj in range(SSD_HPG):
                j = g * SSD_HPG + jj
                seg = jnp.exp(jnp.where(tri, cs[:, j:j + 1] - cs_t[j:j + 1, :], -jnp.inf))
                mcat.append((cb * seg).astype(BF16))
                xbd.append(jnp.where(lane_head == jj, xdt[:, lanes], 0.0).astype(BF16))
            y_diag = jnp.dot(jnp.concatenate(mcat, axis=1), jnp.concatenate(xbd, axis=0), preferred_element_type=F32)
            if d == 0:
                y_diag = y_diag + dskip_ref[:, lanes] * xf[:, lanes]
            st = jnp.dot(bg.astype(F32).T.astype(BF16), xdt_end[:, lanes].astype(BF16), preferred_element_type=F32)
            local.append((cg, y_diag, st))
        states = [dirs[d][3][g] for d in range(2)]
        for (d, rows, tri, cs, cs_t, e_cs, xdt_end, decay, xf, xdt), (cg, y_diag, st) in zip(chains, local):
            y_off = jnp.dot(cg, states[d].astype(BF16), preferred_element_type=F32) * e_cs[:, lanes]
            dirs[d][2][0, rows, lanes] = y_diag + y_off
            states[d] = states[d] * decay[:, lanes] + st
        for d in range(2):
            dirs[d][3][g] = states[d]


def _ssd_scan(xc, dt_c, dt_bias, a_log, d_skip, *, front, group):
    bsz, lp, ch = xc.shape
    rows = group * SSD_CHUNK
    nb = lp // rows

    def spec(width, rev):
        return pl.BlockSpec((1, rows, width), (lambda b, c: (b, nb - 1 - c, 0)) if rev else (lambda b, c: (b, c, 0)))

    aneg = -jnp.exp(a_log)
    pc = jnp.stack([jnp.pad(r, (0, DT_PAD - SSD_HEADS)) for r in (dt_bias[0], aneg[0], dt_bias[1], aneg[1])], axis=0)
    dsk = jnp.repeat(d_skip, SSD_HEAD_DIM).reshape(1, SSD_WIDTH)
    gw = SSD_HPG * SSD_HEAD_DIM
    return pl.pallas_call(
        functools.partial(_ssd_kernel, front=front, group=group), grid=(bsz, nb),
        in_specs=[spec(ch, False), spec(DT_PAD, False), spec(ch, True), spec(DT_PAD, True),
                  _const_spec(pc.shape), _const_spec(dsk.shape)],
        out_specs=[spec(SSD_WIDTH, False), spec(SSD_WIDTH, True)],
        out_shape=[jax.ShapeDtypeStruct((bsz, lp, SSD_WIDTH), F32)] * 2,
        scratch_shapes=[pltpu.VMEM((SSD_GROUPS, SSD_STATE, gw), F32)] * 2,
        compiler_params=_params("parallel", "arbitrary"), name="ssd_scan",
    )(xc, dt_c, xc, dt_c, pc, dsk)


def _head_ones():
    rr, cc = _iota((RWKV_WIDTH, RWKV_WIDTH), 0), _iota((RWKV_WIDTH, RWKV_WIDTH), 1)
    return (jnp.right_shift(rr, 6) == jnp.right_shift(cc, 6)).astype(F32)


def _rwkv_pre_kernel(cur_ref, prev_ref, next_ref, mu_ref, vec_ref, g1_ref, g2_ref, w1_ref, w2_ref, a1_ref,
                     a2_ref, w0a0_ref,
                     r_ref, v_ref, nkk_ref, g_ref, bonus_ref, lw0_ref, kd0_ref, b0_ref, lw1_ref, kd1_ref,
                     b1_ref, *, front):
    j = pl.program_id(1)
    nb = pl.num_programs(1)
    tb = cur_ref.shape[1]
    w = RWKV_WIDTH
    cur = cur_ref[0]
    prow = prev_ref[0, HALO - 1:HALO, :] * jnp.where(j > 0, 1.0, 0.0)
    nrow = next_ref[0, 0:1, :] * jnp.where(j < nb - 1, 1.0, 0.0)
    rowi = _iota((tb, 1), 0)
    prev = jnp.where(rowi == 0, prow, pltpu.roll(cur, 1, 0))
    nxt = jnp.where(rowi == tb - 1, nrow, pltpu.roll(cur, tb - 1, 0))
    shift = 0.5 * (prev + nxt) - cur
    valid = (j * tb + rowi) >= front
    r = cur[:, 0:w] + shift[:, 0:w] * mu_ref[0:1, :]
    k = jnp.where(valid, cur[:, w:2 * w] + shift[:, w:2 * w] * mu_ref[1:2, :], 0.0)
    v = jnp.where(valid, cur[:, 2 * w:3 * w] + shift[:, 2 * w:3 * w] * mu_ref[2:3, :], 0.0)
    xc, dxc = cur[:, 3 * w:], shift[:, 3 * w:]
    xw = xc + dxc * mu_ref[3:4, :]
    xa = xc + dxc * mu_ref[4:5, :]
    xg = xc + dxc * mu_ref[5:6, :]
    k_k, k_a, r_k = vec_ref[0:1, :], vec_ref[1:2, :], vec_ref[2:3, :]
    ones = _head_ones()
    g_ref[0] = _mm(jax.nn.sigmoid(_mm(xg, g1_ref[...])), g2_ref[...])
    kk = k * k_k
    kk = kk * lax.rsqrt(_mm_split(kk * kk, ones) + 1e-12)
    r_ref[0] = r.astype(r_ref.dtype)
    v_ref[0] = v.astype(v_ref.dtype)
    nkk_ref[0] = (-kk).astype(nkk_ref.dtype)
    bonus_ref[0] = _mm_split(r * k * r_k, ones) * v
    for d, (lw_ref, kd_ref, b_ref) in enumerate(((lw0_ref, kd0_ref, b0_ref), (lw1_ref, kd1_ref, b1_ref))):
        lw = w0a0_ref[d:d + 1, :] + _mm(jnp.tanh(_mm(xw, w1_ref[d])), w2_ref[d])
        w_log = -jax.nn.softplus(-lw) - 0.5
        lw_ref[0] = -jnp.exp(w_log)
        ag = jax.nn.sigmoid(w0a0_ref[2 + d:3 + d, :] + _mm(_mm(xa, a1_ref[d]), a2_ref[d]))
        kd_ref[0] = (k * (1.0 + (ag - 1.0) * k_a)).astype(kd_ref.dtype)
        b_ref[0] = (kk * ag).astype(b_ref.dtype)


def _rwkv_pre(rkvx, mu, vec, g1, g2, w1, w2, a1, a2, w0a0, *, front, tb):
    bsz, lp, _ = rkvx.shape
    per = tb // HALO
    nh = lp // HALO
    w = RWKV_WIDTH
    cur = pl.BlockSpec((1, tb, 4 * w), lambda b, j: (b, j, 0))
    prv = pl.BlockSpec((1, HALO, 4 * w), lambda b, j: (b, jnp.maximum(j * per - 1, 0), 0))
    nxt = pl.BlockSpec((1, HALO, 4 * w), lambda b, j: (b, jnp.minimum(j * per + per, nh - 1), 0))
    out = pl.BlockSpec((1, tb, w), lambda b, j: (b, j, 0))
    consts = [mu, vec, g1, g2, w1, w2, a1, a2, w0a0]
    return pl.pallas_call(
        functools.partial(_rwkv_pre_kernel, front=front), grid=(bsz, lp // tb),
        in_specs=[cur, prv, nxt] + [_const_spec(c.shape) for c in consts],
        out_specs=[out] * 11,
        out_shape=[jax.ShapeDtypeStruct((bsz, lp, w), dt) for dt in
                   (BF16, BF16, BF16, F32, F32, F32, BF16, BF16, F32, BF16, BF16)],
        compiler_params=_params("parallel", "parallel"),
        name="rwkv_pre",
    )(rkvx, rkvx, rkvx, *consts)


def _rwkv_chunk_ops(chains):
    n = RWKV_CHUNK
    w = RWKV_WIDTH
    lane_head = jnp.right_shift(_iota((n, w), 1), 6)

    def stack(x):
        return jnp.concatenate([jnp.where(lane_head == h, x, 0.0) for h in range(RWKV_HEADS)], axis=0)

    rr, cc = _iota((w, w), 0), _iota((w, w), 1)
    same = jnp.right_shift(rr, 6) == jnp.right_shift(cc, 6)
    tl, sl = jnp.bitwise_and(rr, n - 1), jnp.bitwise_and(cc, n - 1)
    eye = (rr == cc).astype(F32)
    r8, c8 = _iota((n, n), 0), _iota((n, n), 1)
    masks = {}
    for rev in sorted({ch[6] for ch in chains}):
        masks[rev] = (same & ((sl > tl) if rev else (sl < tl)), same & ((sl >= tl) if rev else (sl <= tl)),
                      (c8 >= r8) if rev else (c8 <= r8))

    pre = []
    for r, v, a, kd, b, lw, rev in chains:
        cum = _mm_cumsum(masks[rev][2], lw)
        tot = cum[0:1, :] if rev else cum[n - 1:n, :]
        p_inv = jnp.exp(-cum)
        p_end = jnp.exp(tot - cum)
        r_t = stack(r * jnp.exp(cum))
        a_t = stack(a * jnp.exp(cum - lw))
        lhs = jnp.concatenate([a_t, r_t], axis=0)
        rhs = jnp.concatenate([stack(b * p_inv), stack(kd * p_inv)], axis=0)
        pre.append((lhs, rhs, r_t, a_t, stack(b * p_end).T, stack(kd * p_end).T, stack(v), jnp.exp(tot)))

    grams = [_mm_nt(p[0], p[1]) for p in pre]
    parts = []
    for ch, gram in zip(chains, grams):
        strict, incl, _ = masks[ch[6]]
        parts.append((jnp.where(strict, gram[0:w, 0:w], 0.0), jnp.where(strict, gram[0:w, w:2 * w], 0.0),
                      jnp.where(incl, gram[w:2 * w, 0:w], 0.0), jnp.where(incl, gram[w:2 * w, w:2 * w], 0.0)))

    def same_block(shift):
        return jnp.right_shift(rr, shift) == jnp.right_shift(cc, shift)

    invs = [eye + jnp.where(same_block(1), p[0], 0.0) for p in parts]
    on_v = [_mm(jnp.concatenate([p[1], p[3], q[5]], axis=0), q[6]) for p, q in zip(parts, pre)]
    for shift in range(1, RWKV_CHUNK.bit_length() - 1):
        between = same_block(shift + 1) & (jnp.right_shift(rr, shift) != jnp.right_shift(cc, shift))
        ys = [_mm(jnp.where(between, p[0], 0.0), inv) for p, inv in zip(parts, invs)]
        invs = [inv + _mm(inv, y) for inv, y in zip(invs, ys)]
    wus = [_mm(inv, jnp.concatenate([q[3], ov[0:w]], axis=1))
           for inv, q, ov in zip(invs, pre, on_v)]
    on_wu = [_mm(jnp.concatenate([p[2], q[4]], axis=0), wu) for p, q, wu in zip(parts, pre, wus)]
    out = []
    for q, ov, ow in zip(pre, on_v, on_wu):
        rp = q[2] + ow[0:w, 0:w]
        y0 = ow[0:w, w:2 * w] + ov[w:2 * w]
        mk = eye * q[7] + ow[w:2 * w, 0:w]
        nk = ow[w:2 * w, w:2 * w] + ov[2 * w:3 * w]
        out.append((y0, rp, mk, nk))
    return out


def _rwkv_scan_kernel(rf_ref, vf_ref, af_ref, lwf_ref, kdf_ref, bf_ref,
                      rb_ref, vb_ref, ab_ref, lwb_ref, kdb_ref, bb_ref,
                      yf_ref, yb_ref, sf_ref, sb_ref, *, group):
    n = RWKV_CHUNK

    @pl.when(pl.program_id(1) == 0)
    def _():
        sf_ref[...] = jnp.zeros_like(sf_ref)
        sb_ref[...] = jnp.zeros_like(sb_ref)

    def unstack(x):
        out = x[0:n]
        for h in range(1, RWKV_HEADS):
            out = out + x[h * n:(h + 1) * n]
        return out

    dirs = (((rf_ref, vf_ref, af_ref, kdf_ref, bf_ref, lwf_ref), yf_ref, sf_ref, False),
            ((rb_ref, vb_ref, ab_ref, kdb_ref, bb_ref, lwb_ref), yb_ref, sb_ref, True))
    chains = []
    for step in range(group):
        for refs, _, _, reverse in dirs:
            gi = group - 1 - step if reverse else step
            chains.append(tuple(ref[0, gi * n:(gi + 1) * n, :].astype(F32) for ref in refs) + (reverse,))
    ops = _rwkv_chunk_ops(chains)
    states = [s_ref[...] for _, _, s_ref, _ in dirs]
    for step in range(group):
        for d, (_, y_ref, _, reverse) in enumerate(dirs):
            gi = group - 1 - step if reverse else step
            y0, rp, mk, nk = ops[step * len(dirs) + d]
            on_st = _mm(jnp.concatenate([rp, mk], axis=0), states[d])
            y_ref[0, gi * n:(gi + 1) * n, :] = unstack(y0 + on_st[0:RWKV_WIDTH])
            states[d] = on_st[RWKV_WIDTH:] + nk
    for d, (_, _, s_ref, _) in enumerate(dirs):
        s_ref[...] = states[d]


def _rwkv_scan(r, v, nkk, lw0, kd0, b0, lw1, kd1, b1, *, group):
    bsz, lp, w = r.shape
    rows = group * RWKV_CHUNK
    nb = lp // rows
    fwd = pl.BlockSpec((1, rows, w), lambda bb, c: (bb, c, 0))
    bwd = pl.BlockSpec((1, rows, w), lambda bb, c: (bb, nb - 1 - c, 0))
    return pl.pallas_call(
        functools.partial(_rwkv_scan_kernel, group=group), grid=(bsz, nb),
        in_specs=[fwd] * 6 + [bwd] * 6, out_specs=[fwd, bwd],
        out_shape=[jax.ShapeDtypeStruct((bsz, lp, w), F32)] * 2,
        scratch_shapes=[pltpu.VMEM((w, w), F32)] * 2,
        compiler_params=_params("parallel", "arbitrary"),
        name="rwkv_scan",
    )(r, v, nkk, lw0, kd0, b0, r, v, nkk, lw1, kd1, b1)


def _merge_kernel(h_ref, gates_ref, ys5_ref, ua_ref, ysf_ref, ysb_ref, z_ref, ycf_ref, ycb_ref, g_ref,
                  bonus_ref, vec_a_ref, glu_w_ref, glu_b_ref, ssd_nw_ref, ln_ref, pa_ref, pb_ref, pc_ref,
                  wo_ref, o_ref):
    ua = ua_ref[...]
    ya = jax.nn.gelu(ys5_ref[...] + vec_a_ref[...] * ua)
    zz = _mm(ya, glu_w_ref[...]) + glu_b_ref[...]
    ya = zz[:, :S5_WIDTH] * jax.nn.sigmoid(zz[:, S5_WIDTH:])
    z = z_ref[...].astype(F32)
    yb = (ysf_ref[...] + ysb_ref[...]) * (z * jax.nn.sigmoid(z))
    yb = yb * lax.rsqrt(jnp.mean(yb * yb, -1, keepdims=True) + EPS) * ssd_nw_ref[...]
    yc = ycf_ref[...] + ycb_ref[...]
    avg = _head_ones() * (1.0 / RWKV_HEAD)
    mean = _mm_split(yc, avg)
    dev = yc - mean
    var = _mm_split(dev * dev, avg)
    yc = dev * lax.rsqrt(var + RWKV_LN_EPS) * ln_ref[0:1, :] + ln_ref[1:2, :]
    yc = (yc + bonus_ref[...]) * g_ref[...]
    d = D_MODEL
    merged = (jax.nn.sigmoid(gates_ref[:, 0:d].astype(F32)) * _mm(ya, pa_ref[...])
              + jax.nn.sigmoid(gates_ref[:, d:2 * d].astype(F32)) * _mm(yb, pb_ref[...])
              + jax.nn.sigmoid(gates_ref[:, 2 * d:3 * d].astype(F32)) * _mm(yc, pc_ref[...]))
    o_ref[...] = h_ref[...] + _mm(merged, wo_ref[...])


def _merge(h2d, gates, ys5, ua, ysf, ysb, z, ycf, ycb, g, bonus, consts, tm):
    t = h2d.shape[0]
    toks = [h2d, gates, ys5, ua, ysf, ysb, z, ycf, ycb, g, bonus]
    return pl.pallas_call(
        _merge_kernel, grid=(t // tm,),
        in_specs=[pl.BlockSpec((tm, a.shape[1]), lambda i: (i, 0)) for a in toks]
        + [_const_spec(c.shape) for c in consts],
        out_specs=pl.BlockSpec((tm, D_MODEL), lambda i: (i, 0)),
        out_shape=jax.ShapeDtypeStruct((t, D_MODEL), F32),
        compiler_params=_params("parallel"),
        name="merge",
    )(*toks, *consts)


def _mlp_kernel(h_ref, nw_ref, w1_ref, w2_ref, o_ref, *, front, rows_per_batch):
    x = h_ref[...]
    tm = x.shape[0]
    xn = x * lax.rsqrt(jnp.mean(x * x, -1, keepdims=True) + EPS) * nw_ref[...]
    a = jnp.maximum(jnp.dot(xn.astype(BF16), w1_ref[...], preferred_element_type=F32), 0.0)
    y = x + jnp.dot((a * a).astype(BF16), w2_ref[...], preferred_element_type=F32)
    row = (pl.program_id(0) * tm) % rows_per_batch + _iota((tm, 1), 0)
    row = jnp.where(row >= rows_per_batch, row - rows_per_batch, row)
    o_ref[...] = jnp.where(row >= front, y, 0.0)


def _mlp_final_kernel(h_ref, nw_ref, w1_ref, w2_ref, fw_ref, o_ref):
    x = h_ref[...]
    xn = x * lax.rsqrt(jnp.mean(x * x, -1, keepdims=True) + EPS) * nw_ref[...]
    a = jnp.maximum(jnp.dot(xn.astype(BF16), w1_ref[...], preferred_element_type=F32), 0.0)
    y = x + jnp.dot((a * a).astype(BF16), w2_ref[...], preferred_element_type=F32)
    o_ref[...] = y * lax.rsqrt(jnp.mean(y * y, -1, keepdims=True) + EPS) * fw_ref[...]


def _mlp_weight_specs():
    return [_const_spec((1, D_MODEL)),
            pl.BlockSpec((D_MODEL, D_FF), lambda *_: (0, 0), pipeline_mode=pl.Buffered(1)),
            pl.BlockSpec((D_FF, D_MODEL), lambda *_: (0, 0), pipeline_mode=pl.Buffered(1))]


def _mlp(h2d, norm_w, w1, w2, *, front, rows_per_batch, tm):
    t = h2d.shape[0]
    assert tm <= rows_per_batch
    kern = functools.partial(_mlp_kernel, front=front, rows_per_batch=rows_per_batch)
    return pl.pallas_call(
        kern, grid=(t // tm,),
        in_specs=[pl.BlockSpec((tm, D_MODEL), lambda i: (i, 0))] + _mlp_weight_specs(),
        out_specs=pl.BlockSpec((tm, D_MODEL), lambda i: (i, 0)),
        out_shape=jax.ShapeDtypeStruct((t, D_MODEL), F32),
        compiler_params=_params("parallel"),
        name="mlp",
    )(h2d, norm_w.reshape(1, D_MODEL), w1.astype(BF16), w2.astype(BF16))


def _mlp_final(h2d, norm_w, w1, w2, final_w, *, bsz, rows_per_batch, first_row, seq, tm):
    assert rows_per_batch % HALO == 0 and first_row % HALO == 0 and tm % HALO == 0

    def rows(b, j):
        return pl.multiple_of(b * rows_per_batch + first_row + j * tm, HALO), 0

    return pl.pallas_call(
        _mlp_final_kernel, grid=(bsz, seq // tm),
        in_specs=[pl.BlockSpec((pl.Element(tm), pl.Element(D_MODEL)), rows)]
        + _mlp_weight_specs() + [_const_spec((1, D_MODEL))],
        out_specs=pl.BlockSpec((None, tm, D_MODEL), lambda b, j: (b, j, 0)),
        out_shape=jax.ShapeDtypeStruct((bsz, seq, D_MODEL), F32),
        compiler_params=_params("parallel", "parallel"),
        name="mlp_final",
    )(h2d, norm_w.reshape(1, D_MODEL), w1.astype(BF16), w2.astype(BF16), final_w.reshape(1, D_MODEL))


def _token_tile(t, cap):
    tm = cap
    while t % tm:
        tm //= 2
    return tm


def _row_tile(lp, cap):
    k = lp // ROW_ALIGN
    best = 1
    for m in range(1, k + 1):
        if k % m == 0 and m * ROW_ALIGN <= cap:
            best = m
    return best * ROW_ALIGN


def kernel(x, meta_tokens, final_norm_w, mix_norm_w, w_in, s5_lambda_re, s5_lambda_im, s5_log_step, s5_b_re, s5_b_im, s5_c_re, s5_c_im, s5_d, s5_glu_w, s5_glu_b, ssd_conv_w, ssd_conv_b, ssd_a_log, ssd_dt_bias, ssd_d, ssd_norm_w, rwkv_mu_rkv, rwkv_mu_wag, rwkv_w0, rwkv_w1, rwkv_w2, rwkv_a0, rwkv_a1, rwkv_a2, rwkv_g1, rwkv_g2, rwkv_k_k, rwkv_k_a, rwkv_r_k, rwkv_ln_w, rwkv_ln_b, proj_a, proj_b, proj_c, w_out, mlp_norm_w, mlp_w1, mlp_w2):
    bsz, seq, d = x.shape
    assert d == D_MODEL
    length = N_META + seq
    lp = -(-length // ROW_ALIGN) * ROW_ALIGN
    front = lp - length
    t = bsz * lp
    depth = w_in.shape[0]
    meta = jnp.broadcast_to(meta_tokens[None].astype(x.dtype), (bsz, N_META, d))
    h = jnp.concatenate([jnp.zeros((bsz, front, d), x.dtype), meta, x], axis=1).reshape(t, d)
    tm = _token_tile(t, 512)
    tb = _row_tile(lp, 512)
    rwkv_group = 3 if (lp // RWKV_CHUNK) % 3 == 0 else 2
    ssd_group = 3 if (lp // SSD_CHUNK) % 3 == 0 else 1

    for i in range(depth):
        xbc, rkvx, gates, z, ua, dt_c = _in_proj(h, mix_norm_w[i], _permute_w_in(w_in[i]), tm)
        tables = _s5_tables(s5_lambda_re[i], s5_lambda_im[i], s5_log_step[i], s5_b_re[i], s5_b_im[i],
                            s5_c_re[i], s5_c_im[i])
        ys5 = _s5_mixer_core(ua.reshape(bsz, lp, S5_WIDTH), tables).reshape(t, S5_WIDTH)
        xc = _ssd_pre(xbc.reshape(bsz, lp, SSD_CONV_CH), ssd_conv_w[i], ssd_conv_b[i], front=front, tb=tb)
        ys = _ssd_scan(xc, dt_c.reshape(bsz, lp, DT_PAD), ssd_dt_bias[i], ssd_a_log[i], ssd_d[i], front=front,
                       group=ssd_group)
        ys = [y.reshape(t, SSD_WIDTH) for y in ys]
        mu = jnp.concatenate([rwkv_mu_rkv[i], rwkv_mu_wag[i]], axis=0)
        vec = jnp.stack([rwkv_k_k[i], rwkv_k_a[i], rwkv_r_k[i].reshape(RWKV_WIDTH)], axis=0)
        w0a0 = jnp.concatenate([rwkv_w0[i], rwkv_a0[i]], axis=0)
        pre = _rwkv_pre(rkvx.reshape(bsz, lp, 4 * RWKV_WIDTH), mu, vec, rwkv_g1[i], rwkv_g2[i], rwkv_w1[i],
                        rwkv_w2[i], rwkv_a1[i], rwkv_a2[i], w0a0, front=front, tb=tb)
        r, v, nkk, g, bonus, lw0, kd0, b0, lw1, kd1, b1 = pre
        ycf, ycb = _rwkv_scan(r, v, nkk, lw0, kd0, b0, lw1, kd1, b1, group=rwkv_group)
        ycf, ycb = ycf.reshape(t, RWKV_WIDTH), ycb.reshape(t, RWKV_WIDTH)
        consts = [s5_d[i].reshape(1, S5_WIDTH), s5_glu_w[i].astype(BF16), s5_glu_b[i].reshape(1, 2 * S5_WIDTH),
                  ssd_norm_w[i].reshape(1, SSD_WIDTH), jnp.stack([rwkv_ln_w[i], rwkv_ln_b[i]], axis=0),
                  proj_a[i].astype(BF16), proj_b[i].astype(BF16), proj_c[i].astype(BF16),
                  w_out[i].astype(BF16)]
        h = _merge(h, gates, ys5, ua, ys[0], ys[1], z, ycf, ycb, g.reshape(t, RWKV_WIDTH),
                   bonus.reshape(t, RWKV_WIDTH), consts, tm)
        if i < depth - 1:
            h = _mlp(h, mlp_norm_w[i], mlp_w1[i], mlp_w2[i], front=front, rows_per_batch=lp, tm=tm)
    return _mlp_final(h, mlp_norm_w[depth - 1], mlp_w1[depth - 1], mlp_w2[depth - 1], final_norm_w, bsz=bsz,
                      rows_per_batch=lp, first_row=front + N_META, seq=seq, tm=_token_tile(seq, 512))
```

```python
import functools

import jax
import jax.numpy as jnp
from jax import lax
from jax.experimental import pallas as pl
from jax.experimental.pallas import tpu as pltpu

D_MODEL = 1024
N_META = 16
EPS = 1e-6
D_FF = 4 * D_MODEL

S5_WIDTH = 256
S5_GROUP = 16
S5_GROUPS = 16
S5_STATE = 64
S5_Q = 16

SSD_WIDTH = 512
SSD_HEAD_DIM = 64
SSD_HEADS = 8
SSD_GROUPS = 2
SSD_HPG = SSD_HEADS // SSD_GROUPS
SSD_STATE = 128
SSD_CONV = 5
SSD_CHUNK = 128
SSD_CONV_CH = SSD_WIDTH + 2 * SSD_GROUPS * SSD_STATE

RWKV_WIDTH = 256
RWKV_HEAD = 64
RWKV_HEADS = 4
RWKV_LN_EPS = 64e-5
RWKV_CHUNK = 64

N_BRANCH = 3
DT_PAD = 128
ROW_ALIGN = 128
HALO = 8

F32 = jnp.float32
BF16 = jnp.bfloat16
VMEM_LIMIT = 56 * 1024 * 1024


def _mm(a, b):
    return jnp.dot(a.astype(BF16), b.astype(BF16), preferred_element_type=F32)


def _bf16_parts(a, parts):
    out = []
    rest = a
    for _ in range(parts):
        part = rest.astype(BF16)
        rest = rest - part.astype(F32)
        out.append(part)
    return out


def _mm_split(a, b, parts=2):
    bb = b.astype(BF16)
    terms = [jnp.dot(p, bb, preferred_element_type=F32) for p in _bf16_parts(a, parts)]
    return functools.reduce(lambda x, y: x + y, terms)


def _mm_cumsum(tri, a):
    t = tri.astype(BF16)
    terms = [jnp.dot(t, p, preferred_element_type=F32) for p in _bf16_parts(a, 3)]
    return functools.reduce(lambda x, y: x + y, terms)


def _mm_nt(a, b):
    return lax.dot_general(a.astype(BF16), b.astype(BF16), (((1,), (1,)), ((), ())),
                           preferred_element_type=F32)


def _iota(shape, dim):
    return lax.broadcasted_iota(jnp.int32, shape, dim)


def _params(*sem):
    return pltpu.CompilerParams(dimension_semantics=sem, vmem_limit_bytes=VMEM_LIMIT)


def _const_spec(shape):
    nd = len(shape)
    return pl.BlockSpec(shape, lambda *_: (0,) * nd)


def _in_proj_kernel(h_ref, nw_ref, w_ref, xbc_ref, rkvx_ref, gates_ref, z_ref, ua_ref, dt_ref):
    x = h_ref[...]
    xn = x * lax.rsqrt(jnp.mean(x * x, -1, keepdims=True) + EPS) * nw_ref[...]
    xb = xn.astype(BF16)
    outs = ((xbc_ref, None), (rkvx_ref, None), (gates_ref, jax.nn.sigmoid), (z_ref, jax.nn.silu), (ua_ref, None),
            (dt_ref, None))
    off = 0
    for ref, act in outs:
        n = ref.shape[-1]
        y = jnp.dot(xb, w_ref[:, off:off + n], preferred_element_type=F32)
        ref[...] = (y if act is None else act(y)).astype(ref.dtype)
        off += n


def _in_proj(h2d, norm_w, w_perm, tm):
    t = h2d.shape[0]
    widths = (SSD_CONV_CH, 4 * RWKV_WIDTH, N_BRANCH * D_MODEL, SSD_WIDTH, S5_WIDTH, DT_PAD)
    dtypes = (F32, F32, BF16, BF16, F32, F32)
    n_all = sum(widths)
    return pl.pallas_call(
        _in_proj_kernel,
        grid=(t // tm,),
        in_specs=[pl.BlockSpec((tm, D_MODEL), lambda i: (i, 0)),
                  _const_spec((1, D_MODEL)),
                  pl.BlockSpec((D_MODEL, n_all), lambda i: (0, 0), pipeline_mode=pl.Buffered(1))],
        out_specs=[pl.BlockSpec((tm, n), lambda i: (i, 0)) for n in widths],
        out_shape=[jax.ShapeDtypeStruct((t, n), dt) for n, dt in zip(widths, dtypes)],
        compiler_params=_params("parallel"),
        name="in_proj",
    )(h2d, norm_w.reshape(1, D_MODEL), w_perm)


def _permute_w_in(w):
    o_u, o_z, o_xbc = 0, S5_WIDTH, S5_WIDTH + SSD_WIDTH
    o_dt = o_xbc + SSD_CONV_CH
    o_rkvx = o_dt + SSD_HEADS
    o_g = o_rkvx + 4 * RWKV_WIDTH
    dt_cols = jnp.pad(w[:, o_dt:o_rkvx], ((0, 0), (0, DT_PAD - SSD_HEADS)))
    return jnp.concatenate([w[:, o_xbc:o_dt], w[:, o_rkvx:o_g], w[:, o_g:], w[:, o_z:o_xbc],
                            w[:, o_u:o_z], dt_cols], axis=1).astype(BF16)


def _s5_tables(lam_re, lam_im, log_step, b_re, b_im, c_re, c_im):
    g, n, hh = S5_GROUPS, S5_STATE, S5_GROUP
    step = jnp.exp(log_step)[:, :, None]

    def power(j):
        mag = jnp.exp(lam_re * step * j)
        return mag * jnp.cos(lam_im * step * j), mag * jnp.sin(lam_im * step * j)

    ab_re, ab_im = power(1.0)
    aq_re, aq_im = power(float(S5_Q))
    den = lam_re * lam_re + lam_im * lam_im
    co_re = ((ab_re - 1.0) * lam_re + ab_im * lam_im) / den
    co_im = (ab_im * lam_re - (ab_re - 1.0) * lam_im) / den
    eb_re = co_re[..., None] * b_re - co_im[..., None] * b_im
    eb_im = co_re[..., None] * b_im + co_im[..., None] * b_re
    eye = jnp.eye(g, dtype=F32)
    b_in = jnp.stack([jnp.einsum('dgni,gk->dgikn', e, eye) for e in (eb_re, eb_im)], axis=3)
    b_in = b_in.reshape(2, g * hh, 2 * g * n).astype(BF16)
    c_out = jnp.stack([jnp.einsum('gon,gk->gnko', c, eye) for c in (c_re, -c_im)], axis=0)
    c_out = c_out.reshape(2 * g * n, g * hh).astype(BF16)
    rows = [ab_re[0], ab_im[0], ab_re[1], ab_im[1], aq_re[0], aq_im[0], aq_re[1], aq_im[1]]
    return b_in, c_out, jnp.stack([r.reshape(g * n) for r in rows], axis=0)


def _s5_kernel(u_ref, bin_ref, cout_ref, a_ref, y_ref, h_ref, bnd_ref, us_ref, ys_ref):
    q = S5_Q
    nc = u_ref.shape[1] // q
    gn = S5_GROUPS * S5_STATE
    half = S5_WIDTH // 2
    for hh in range(2):
        us_ref[hh] = u_ref[0, :, hh * half:(hh + 1) * half]

    def token_rows(ref, hh, t):
        return ref.at[hh, pl.ds(t, nc, stride=q), :]

    def advance(i):
        for d, t in ((0, i), (1, q - 1 - i)):
            ut = jnp.concatenate([token_rows(us_ref, hh, t)[...] for hh in range(2)], axis=1)
            e = jnp.dot(ut.astype(BF16), bin_ref[d], preferred_element_type=F32)
            a_re, a_im = a_ref[2 * d:2 * d + 1, :], a_ref[2 * d + 1:2 * d + 2, :]
            h_re, h_im = h_ref[2 * d], h_ref[2 * d + 1]
            h_ref[2 * d] = a_re * h_re - a_im * h_im + e[:, 0:gn]
            h_ref[2 * d + 1] = a_re * h_im + a_im * h_re + e[:, gn:2 * gn]

    h_ref[...] = jnp.zeros_like(h_ref)

    def local_step(i, carry):
        advance(i)
        return carry

    lax.fori_loop(0, q, local_step, 0)

    sub = _iota((HALO, gn), 0)

    def boundary(i, carry):
        starts = (pl.multiple_of(i * HALO, HALO), pl.multiple_of(nc - HALO - i * HALO, HALO))
        new = []
        for d in range(2):
            h_re, h_im = carry[2 * d], carry[2 * d + 1]
            a_re, a_im = a_ref[4 + 2 * d:5 + 2 * d, :], a_ref[5 + 2 * d:6 + 2 * d, :]
            s_re = h_ref[2 * d, pl.ds(starts[d], HALO), :]
            s_im = h_ref[2 * d + 1, pl.ds(starts[d], HALO), :]
            out_re = jnp.zeros((HALO, gn), F32)
            out_im = jnp.zeros((HALO, gn), F32)
            for step in range(HALO):
                rix = HALO - 1 - step if d else step
                out_re = jnp.where(sub == rix, h_re, out_re)
                out_im = jnp.where(sub == rix, h_im, out_im)
                n_re = a_re * h_re - a_im * h_im + s_re[rix:rix + 1, :]
                n_im = a_re * h_im + a_im * h_re + s_im[rix:rix + 1, :]
                h_re, h_im = n_re, n_im
            bnd_ref[2 * d, pl.ds(starts[d], HALO), :] = out_re
            bnd_ref[2 * d + 1, pl.ds(starts[d], HALO), :] = out_im
            new += [h_re, h_im]
        return tuple(new)

    zero = jnp.zeros((1, gn), F32)
    lax.fori_loop(0, nc // HALO, boundary, (zero,) * 4)

    h_ref[...] = bnd_ref[...]

    def emit(i, first):
        advance(i)
        for d, t in ((0, i), (1, q - 1 - i)):
            hcat = jnp.concatenate([h_ref[2 * d], h_ref[2 * d + 1]], axis=1).astype(BF16)
            part = jnp.dot(hcat, cout_ref[...], preferred_element_type=F32)
            for hh in range(2):
                rows = token_rows(ys_ref, hh, t)
                piece = part[:, hh * half:(hh + 1) * half]
                rows[...] = piece if first else rows[...] + piece

    def emit_first(i, carry):
        emit(i, True)
        return carry

    def emit_second(i, carry):
        emit(i, False)
        return carry

    lax.fori_loop(0, q // 2, emit_first, 0)
    lax.fori_loop(q // 2, q, emit_second, 0)
    for hh in range(2):
        y_ref[0, :, hh * half:(hh + 1) * half] = ys_ref[hh]


def _s5_mixer_core(u, tables):
    b_in, c_out, a_rows = tables
    bsz, lp, width = u.shape
    nc = lp // S5_Q
    gn = S5_GROUPS * S5_STATE
    row = pl.BlockSpec((1, lp, width), lambda b: (b, 0, 0))
    return pl.pallas_call(
        _s5_kernel, grid=(bsz,),
        in_specs=[row, _const_spec(b_in.shape), _const_spec(c_out.shape), _const_spec(a_rows.shape)],
        out_specs=row, out_shape=jax.ShapeDtypeStruct((bsz, lp, width), F32),
        scratch_shapes=[pltpu.VMEM((4, nc, gn), F32)] * 2 + [pltpu.VMEM((2, lp, width // 2), F32)] * 2,
        compiler_params=_params("parallel"), name="s5_scan")(u, b_in, c_out, a_rows)


def _ssd_pre_kernel(cur_ref, prev_ref, next_ref, cw_ref, cb_ref, xc_ref, *, front):
    j = pl.program_id(1)
    nb = pl.num_programs(1)
    tb = cur_ref.shape[1]
    ext = jnp.concatenate([prev_ref[0] * jnp.where(j > 0, 1.0, 0.0), cur_ref[0],
                           next_ref[0] * jnp.where(j < nb - 1, 1.0, 0.0)], axis=0)
    n = tb + 2 * HALO
    lo = pltpu.roll(pltpu.roll(cw_ref[0:1, :] * ext, 1, 0) + cw_ref[1:2, :] * ext, 1, 0)
    hi = pltpu.roll(pltpu.roll(cw_ref[4:5, :] * ext, n - 1, 0) + cw_ref[3:4, :] * ext, n - 1, 0)
    acc = (lo + hi + cw_ref[2:3, :] * ext)[HALO:HALO + tb] + cb_ref[...]
    valid = (j * tb + _iota((tb, 1), 0)) >= front
    xc_ref[0] = jnp.where(valid, acc * jax.nn.sigmoid(acc), 0.0).astype(BF16)


def _ssd_pre(xbc, conv_w, conv_b, *, front, tb):
    bsz, lp, ch = xbc.shape
    per = tb // HALO
    nh = lp // HALO
    cur = pl.BlockSpec((1, tb, ch), lambda b, j: (b, j, 0))
    prv = pl.BlockSpec((1, HALO, ch), lambda b, j: (b, jnp.maximum(j * per - 1, 0), 0))
    nxt = pl.BlockSpec((1, HALO, ch), lambda b, j: (b, jnp.minimum(j * per + per, nh - 1), 0))
    cw = jnp.pad(conv_w, ((0, HALO - SSD_CONV), (0, 0)))
    return pl.pallas_call(
        functools.partial(_ssd_pre_kernel, front=front), grid=(bsz, lp // tb),
        in_specs=[cur, prv, nxt, _const_spec((HALO, ch)), _const_spec((1, ch))],
        out_specs=cur, out_shape=jax.ShapeDtypeStruct((bsz, lp, ch), BF16),
        compiler_params=_params("parallel", "parallel"), name="ssd_pre",
    )(xbc, xbc, xbc, cw, conv_b.reshape(1, ch))


def _ssd_kernel(xcf_ref, dcf_ref, xcb_ref, dcb_ref, pc_ref, dskip_ref, yf_ref, yb_ref, hf_ref, hb_ref, *,
                front, group):
    c = pl.program_id(1)
    nb = pl.num_programs(1)
    q = SSD_CHUNK
    gw = SSD_HPG * SSD_HEAD_DIM

    @pl.when(c == 0)
    def _():
        hf_ref[...] = jnp.zeros_like(hf_ref)
        hb_ref[...] = jnp.zeros_like(hb_ref)

    rr, cc = _iota((q, q), 0), _iota((q, q), 1)
    rowi = _iota((q, 1), 0)
    lane_head = jnp.right_shift(_iota((q, gw), 1), 6)
    spread = jnp.right_shift(_iota((DT_PAD, SSD_WIDTH), 1), 6) == _iota((DT_PAD, SSD_WIDTH), 0)
    dirs = ((xcf_ref, dcf_ref, yf_ref, hf_ref, False, c), (xcb_ref, dcb_ref, yb_ref, hb_ref, True, nb - 1 - c))

    chains = []
    for step in range(group):
        for d, (xc_ref, dc_ref, _, _, rev, blk) in enumerate(dirs):
            gi = group - 1 - step if rev else step
            rows = slice(gi * q, (gi + 1) * q)
            valid = ((blk * group + gi) * q + rowi) >= front
            tri = (cc >= rr) if rev else (cc <= rr)
            dt = jnp.where(valid, jax.nn.softplus(dc_ref[0, rows, :] + pc_ref[2 * d:2 * d + 1, :]), 0.0)
            cs = _mm_cumsum(tri, dt * pc_ref[2 * d + 1:2 * d + 2, :])
            tot = cs[0:1, :] if rev else cs[q - 1:q, :]
            wide = _mm_split(jnp.concatenate([dt, jnp.exp(cs), jnp.exp(tot - cs)], axis=0), spread, parts=3)
            dt_e, e_cs, to_end = wide[0:q], wide[q:2 * q], wide[2 * q:3 * q]
            decay = e_cs[0:1, :] if rev else e_cs[q - 1:q, :]
            xf = xc_ref[0, rows, 0:SSD_WIDTH].astype(F32)
            xdt = xf * dt_e
            chains.append((d, rows, tri, cs, cs.T, e_cs, xdt * to_end, decay, xf, xdt))

    for g in range(SSD_GROUPS):
        lanes = slice(g * gw, (g + 1) * gw)
        b0 = SSD_WIDTH + g * SSD_STATE
        c0 = SSD_WIDTH + (SSD_GROUPS + g) * SSD_STATE
        local = []
        for d, rows, tri, cs, cs_t, e_cs, xdt_end, decay, xf, xdt in chains:
            xc_ref = dirs[d][0]
            bg = xc_ref[0, rows, b0:b0 + SSD_STATE]
            cg = xc_ref[0, rows, c0:c0 + SSD_STATE]
            cb = lax.dot_general(cg, bg, (((1,), (1,)), ((), ())), preferred_element_type=F32)
            mcat, xbd = [], []
            for jj in range(SSD_HPG):
                j = g * SSD_HPG + jj
                seg = jnp.exp(jnp.where(tri, cs[:, j:j + 1] - cs_t[j:j + 1, :], -jnp.inf))
                mcat.append((cb * seg).astype(BF16))
                xbd.append(jnp.where(lane_head == jj, xdt[:, lanes], 0.0).astype(BF16))
            y_diag = jnp.dot(jnp.concatenate(mcat, axis=1), jnp.concatenate(xbd, axis=0), preferred_element_type=F32)
            if d == 0:
                y_diag = y_diag + dskip_ref[:, lanes] * xf[:, lanes]
            st = jnp.dot(bg.astype(F32).T.astype(BF16), xdt_end[:, lanes].astype(BF16), preferred_element_type=F32)
            local.append((cg, y_diag, st))
        states = [dirs[d][3][g] for d in range(2)]
        for (d, rows, tri, cs, cs_t, e_cs, xdt_end, decay, xf, xdt), (cg, y_diag, st) in zip(chains, local):
            y_off = jnp.dot(cg, states[d].astype(BF16), preferred_element_type=F32) * e_cs[:, lanes]
            dirs[d][2][0, rows, lanes] = y_diag + y_off
            states[d] = states[d] * decay[:, lanes] + st
        for d in range(2):
            dirs[d][3][g] = states[d]


def _ssd_scan(xc, dt_c, dt_bias, a_log, d_skip, *, front, group):
    bsz, lp, ch = xc.shape
    rows = group * SSD_CHUNK
    nb = lp // rows

    def spec(width, rev):
        return pl.BlockSpec((1, rows, width), (lambda b, c: (b, nb - 1 - c, 0)) if rev else (lambda b, c: (b, c, 0)))

    aneg = -jnp.exp(a_log)
    pc = jnp.stack([jnp.pad(r, (0, DT_PAD - SSD_HEADS)) for r in (dt_bias[0], aneg[0], dt_bias[1], aneg[1])], axis=0)
    dsk = jnp.repeat(d_skip, SSD_HEAD_DIM).reshape(1, SSD_WIDTH)
    gw = SSD_HPG * SSD_HEAD_DIM
    return pl.pallas_call(
        functools.partial(_ssd_kernel, front=front, group=group), grid=(bsz, nb),
        in_specs=[spec(ch, False), spec(DT_PAD, False), spec(ch, True), spec(DT_PAD, True),
                  _const_spec(pc.shape), _const_spec(dsk.shape)],
        out_specs=[spec(SSD_WIDTH, False), spec(SSD_WIDTH, True)],
        out_shape=[jax.ShapeDtypeStruct((bsz, lp, SSD_WIDTH), F32)] * 2,
        scratch_shapes=[pltpu.VMEM((SSD_GROUPS, SSD_STATE, gw), F32)] * 2,
        compiler_params=_params("parallel", "arbitrary"), name="ssd_scan",
    )(xc, dt_c, xc, dt_c, pc, dsk)


def _head_ones():
    rr, cc = _iota((RWKV_WIDTH, RWKV_WIDTH), 0), _iota((RWKV_WIDTH, RWKV_WIDTH), 1)
    return (jnp.right_shift(rr, 6) == jnp.right_shift(cc, 6)).astype(F32)


def _rwkv_pre_kernel(cur_ref, prev_ref, next_ref, mu_ref, vec_ref, g1_ref, g2_ref, w1_ref, w2_ref, a1_ref,
                     a2_ref, w0a0_ref,
                     r_ref, v_ref, nkk_ref, g_ref, bonus_ref, lw0_ref, kd0_ref, b0_ref, lw1_ref, kd1_ref,
                     b1_ref, *, front):
    j = pl.program_id(1)
    nb = pl.num_programs(1)
    tb = cur_ref.shape[1]
    w = RWKV_WIDTH
    cur = cur_ref[0]
    prow = prev_ref[0, HALO - 1:HALO, :] * jnp.where(j > 0, 1.0, 0.0)
    nrow = next_ref[0, 0:1, :] * jnp.where(j < nb - 1, 1.0, 0.0)
    rowi = _iota((tb, 1), 0)
    prev = jnp.where(rowi == 0, prow, pltpu.roll(cur, 1, 0))
    nxt = jnp.where(rowi == tb - 1, nrow, pltpu.roll(cur, tb - 1, 0))
    shift = 0.5 * (prev + nxt) - cur
    valid = (j * tb + rowi) >= front
    r = cur[:, 0:w] + shift[:, 0:w] * mu_ref[0:1, :]
    k = jnp.where(valid, cur[:, w:2 * w] + shift[:, w:2 * w] * mu_ref[1:2, :], 0.0)
    v = jnp.where(valid, cur[:, 2 * w:3 * w] + shift[:, 2 * w:3 * w] * mu_ref[2:3, :], 0.0)
    xc, dxc = cur[:, 3 * w:], shift[:, 3 * w:]
    xw = xc + dxc * mu_ref[3:4, :]
    xa = xc + dxc * mu_ref[4:5, :]
    xg = xc + dxc * mu_ref[5:6, :]
    k_k, k_a, r_k = vec_ref[0:1, :], vec_ref[1:2, :], vec_ref[2:3, :]
    ones = _head_ones()
    g_ref[0] = _mm(jax.nn.sigmoid(_mm(xg, g1_ref[...])), g2_ref[...])
    kk = k * k_k
    kk = kk * lax.rsqrt(_mm_split(kk * kk, ones) + 1e-12)
    r_ref[0] = r.astype(r_ref.dtype)
    v_ref[0] = v.astype(v_ref.dtype)
    nkk_ref[0] = (-kk).astype(nkk_ref.dtype)
    bonus_ref[0] = _mm_split(r * k * r_k, ones) * v
    for d, (lw_ref, kd_ref, b_ref) in enumerate(((lw0_ref, kd0_ref, b0_ref), (lw1_ref, kd1_ref, b1_ref))):
        lw = w0a0_ref[d:d + 1, :] + _mm(jnp.tanh(_mm(xw, w1_ref[d])), w2_ref[d])
        w_log = -jax.nn.softplus(-lw) - 0.5
        lw_ref[0] = -jnp.exp(w_log)
        ag = jax.nn.sigmoid(w0a0_ref[2 + d:3 + d, :] + _mm(_mm(xa, a1_ref[d]), a2_ref[d]))
        kd_ref[0] = (k * (1.0 + (ag - 1.0) * k_a)).astype(kd_ref.dtype)
        b_ref[0] = (kk * ag).astype(b_ref.dtype)


def _rwkv_pre(rkvx, mu, vec, g1, g2, w1, w2, a1, a2, w0a0, *, front, tb):
    bsz, lp, _ = rkvx.shape
    per = tb // HALO
    nh = lp // HALO
    w = RWKV_WIDTH
    cur = pl.BlockSpec((1, tb, 4 * w), lambda b, j: (b, j, 0))
    prv = pl.BlockSpec((1, HALO, 4 * w), lambda b, j: (b, jnp.maximum(j * per - 1, 0), 0))
    nxt = pl.BlockSpec((1, HALO, 4 * w), lambda b, j: (b, jnp.minimum(j * per + per, nh - 1), 0))
    out = pl.BlockSpec((1, tb, w), lambda b, j: (b, j, 0))
    consts = [mu, vec, g1, g2, w1, w2, a1, a2, w0a0]
    return pl.pallas_call(
        functools.partial(_rwkv_pre_kernel, front=front), grid=(bsz, lp // tb),
        in_specs=[cur, prv, nxt] + [_const_spec(c.shape) for c in consts],
        out_specs=[out] * 11,
        out_shape=[jax.ShapeDtypeStruct((bsz, lp, w), dt) for dt in
                   (BF16, BF16, BF16, F32, F32, F32, BF16, BF16, F32, BF16, BF16)],
        compiler_params=_params("parallel", "parallel"),
        name="rwkv_pre",
    )(rkvx, rkvx, rkvx, *consts)


def _rwkv_chunk_ops(chains):
    n = RWKV_CHUNK
    w = RWKV_WIDTH
    lane_head = jnp.right_shift(_iota((n, w), 1), 6)

    def stack(x):
        return jnp.concatenate([jnp.where(lane_head == h, x, 0.0) for h in range(RWKV_HEADS)], axis=0)

    rr, cc = _iota((w, w), 0), _iota((w, w), 1)
    same = jnp.right_shift(rr, 6) == jnp.right_shift(cc, 6)
    tl, sl = jnp.bitwise_and(rr, n - 1), jnp.bitwise_and(cc, n - 1)
    eye = (rr == cc).astype(F32)
    r8, c8 = _iota((n, n), 0), _iota((n, n), 1)
    masks = {}
    for rev in sorted({ch[6] for ch in chains}):
        masks[rev] = (same & ((sl > tl) if rev else (sl < tl)), same & ((sl >= tl) if rev else (sl <= tl)),
                      (c8 >= r8) if rev else (c8 <= r8))

    pre = []
    for r, v, a, kd, b, lw, rev in chains:
        cum = _mm_cumsum(masks[rev][2], lw)
        tot = cum[0:1, :] if rev else cum[n - 1:n, :]
        p_inv = jnp.exp(-cum)
        p_end = jnp.exp(tot - cum)
        r_t = stack(r * jnp.exp(cum))
        a_t = stack(a * jnp.exp(cum - lw))
        lhs = jnp.concatenate([a_t, r_t], axis=0)
        rhs = jnp.concatenate([stack(b * p_inv), stack(kd * p_inv)], axis=0)
        pre.append((lhs, rhs, r_t, a_t, stack(b * p_end).T, stack(kd * p_end).T, stack(v), jnp.exp(tot)))

    grams = [_mm_nt(p[0], p[1]) for p in pre]
    parts = []
    for ch, gram in zip(chains, grams):
        strict, incl, _ = masks[ch[6]]
        parts.append((jnp.where(strict, gram[0:w, 0:w], 0.0), jnp.where(strict, gram[0:w, w:2 * w], 0.0),
                      jnp.where(incl, gram[w:2 * w, 0:w], 0.0), jnp.where(incl, gram[w:2 * w, w:2 * w], 0.0)))

    def same_block(shift):
        return jnp.right_shift(rr, shift) == jnp.right_shift(cc, shift)

    invs = [eye + jnp.where(same_block(1), p[0], 0.0) for p in parts]
    on_v = [_mm(jnp.concatenate([p[1], p[3], q[5]], axis=0), q[6]) for p, q in zip(parts, pre)]
    for shift in range(1, RWKV_CHUNK.bit_length() - 1):
        between = same_block(shift + 1) & (jnp.right_shift(rr, shift) != jnp.right_shift(cc, shift))
        ys = [_mm(jnp.where(between, p[0], 0.0), inv) for p, inv in zip(parts, invs)]
        invs = [inv + _mm(inv, y) for inv, y in zip(invs, ys)]
    wus = [_mm(inv, jnp.concatenate([q[3], ov[0:w]], axis=1))
           for inv, q, ov in zip(invs, pre, on_v)]
    on_wu = [_mm(jnp.concatenate([p[2], q[4]], axis=0), wu) for p, q, wu in zip(parts, pre, wus)]
    out = []
    for q, ov, ow in zip(pre, on_v, on_wu):
        rp = q[2] + ow[0:w, 0:w]
        y0 = ow[0:w, w:2 * w] + ov[w:2 * w]
        mk = eye * q[7] + ow[w:2 * w, 0:w]
        nk = ow[w:2 * w, w:2 * w] + ov[2 * w:3 * w]
        out.append((y0, rp, mk, nk))
    return out


def _rwkv_scan_kernel(rf_ref, vf_ref, af_ref, lwf_ref, kdf_ref, bf_ref,
                      rb_ref, vb_ref, ab_ref, lwb_ref, kdb_ref, bb_ref,
                      yf_ref, yb_ref, sf_ref, sb_ref, *, group):
    n = RWKV_CHUNK

    @pl.when(pl.program_id(1) == 0)
    def _():
        sf_ref[...] = jnp.zeros_like(sf_ref)
        sb_ref[...] = jnp.zeros_like(sb_ref)

    def unstack(x):
        out = x[0:n]
        for h in range(1, RWKV_HEADS):
            out = out + x[h * n:(h + 1) * n]
        return out

    dirs = (((rf_ref, vf_ref, af_ref, kdf_ref, bf_ref, lwf_ref), yf_ref, sf_ref, False),
            ((rb_ref, vb_ref, ab_ref, kdb_ref, bb_ref, lwb_ref), yb_ref, sb_ref, True))
    chains = []
    for step in range(group):
        for refs, _, _, reverse in dirs:
            gi = group - 1 - step if reverse else step
            chains.append(tuple(ref[0, gi * n:(gi + 1) * n, :].astype(F32) for ref in refs) + (reverse,))
    ops = _rwkv_chunk_ops(chains)
    states = [s_ref[...] for _, _, s_ref, _ in dirs]
    for step in range(group):
        for d, (_, y_ref, _, reverse) in enumerate(dirs):
            gi = group - 1 - step if reverse else step
            y0, rp, mk, nk = ops[step * len(dirs) + d]
            on_st = _mm(jnp.concatenate([rp, mk], axis=0), states[d])
            y_ref[0, gi * n:(gi + 1) * n, :] = unstack(y0 + on_st[0:RWKV_WIDTH])
            states[d] = on_st[RWKV_WIDTH:] + nk
    for d, (_, _, s_ref, _) in enumerate(dirs):
        s_ref[...] = states[d]


def _rwkv_scan(r, v, nkk, lw0, kd0, b0, lw1, kd1, b1, *, group):
    bsz, lp, w = r.shape
    rows = group * RWKV_CHUNK
    nb = lp // rows
    fwd = pl.BlockSpec((1, rows, w), lambda bb, c: (bb, c, 0))
    bwd = pl.BlockSpec((1, rows, w), lambda bb, c: (bb, nb - 1 - c, 0))
    return pl.pallas_call(
        functools.partial(_rwkv_scan_kernel, group=group), grid=(bsz, nb),
        in_specs=[fwd] * 6 + [bwd] * 6, out_specs=[fwd, bwd],
        out_shape=[jax.ShapeDtypeStruct((bsz, lp, w), F32)] * 2,
        scratch_shapes=[pltpu.VMEM((w, w), F32)] * 2,
        compiler_params=_params("parallel", "arbitrary"),
        name="rwkv_scan",
    )(r, v, nkk, lw0, kd0, b0, r, v, nkk, lw1, kd1, b1)


def _merge_kernel(h_ref, gates_ref, ys5_ref, ua_ref, ysf_ref, ysb_ref, z_ref, ycf_ref, ycb_ref, g_ref,
                  bonus_ref, vec_a_ref, glu_w_ref, glu_b_ref, ssd_nw_ref, ln_ref, pa_ref, pb_ref, pc_ref,
                  wo_ref, o_ref):
    ua = ua_ref[...]
    ya = jax.nn.gelu(ys5_ref[...] + vec_a_ref[...] * ua)
    zz = _mm(ya, glu_w_ref[...]) + glu_b_ref[...]
    ya = zz[:, :S5_WIDTH] * jax.nn.sigmoid(zz[:, S5_WIDTH:])
    yb = (ysf_ref[...] + ysb_ref[...]) * z_ref[...].astype(F32)
    yb = yb * lax.rsqrt(jnp.mean(yb * yb, -1, keepdims=True) + EPS) * ssd_nw_ref[...]
    yc = ycf_ref[...] + ycb_ref[...]
    avg = _head_ones() * (1.0 / RWKV_HEAD)
    mean = _mm_split(yc, avg)
    dev = yc - mean
    var = _mm_split(dev * dev, avg)
    yc = dev * lax.rsqrt(var + RWKV_LN_EPS) * ln_ref[0:1, :] + ln_ref[1:2, :]
    yc = (yc + bonus_ref[...]) * g_ref[...]
    d = D_MODEL
    merged = (gates_ref[:, 0:d].astype(F32) * _mm(ya, pa_ref[...])
              + gates_ref[:, d:2 * d].astype(F32) * _mm(yb, pb_ref[...])
              + gates_ref[:, 2 * d:3 * d].astype(F32) * _mm(yc, pc_ref[...]))
    o_ref[...] = h_ref[...] + _mm(merged, wo_ref[...])


def _merge(h2d, gates, ys5, ua, ysf, ysb, z, ycf, ycb, g, bonus, consts, tm):
    t = h2d.shape[0]
    toks = [h2d, gates, ys5, ua, ysf, ysb, z, ycf, ycb, g, bonus]
    return pl.pallas_call(
        _merge_kernel, grid=(t // tm,),
        in_specs=[pl.BlockSpec((tm, a.shape[1]), lambda i: (i, 0)) for a in toks]
        + [_const_spec(c.shape) for c in consts],
        out_specs=pl.BlockSpec((tm, D_MODEL), lambda i: (i, 0)),
        out_shape=jax.ShapeDtypeStruct((t, D_MODEL), F32),
        compiler_params=_params("parallel"),
        name="merge",
    )(*toks, *consts)


def _mlp_kernel(h_ref, nw_ref, w1_ref, w2_ref, o_ref, *, front, rows_per_batch):
    x = h_ref[...]
    tm = x.shape[0]
    xn = x * lax.rsqrt(jnp.mean(x * x, -1, keepdims=True) + EPS) * nw_ref[...]
    a = jnp.maximum(jnp.dot(xn.astype(BF16), w1_ref[...], preferred_element_type=F32), 0.0)
    y = x + jnp.dot((a * a).astype(BF16), w2_ref[...], preferred_element_type=F32)
    row = (pl.program_id(0) * tm) % rows_per_batch + _iota((tm, 1), 0)
    row = jnp.where(row >= rows_per_batch, row - rows_per_batch, row)
    o_ref[...] = jnp.where(row >= front, y, 0.0)


def _mlp_final_kernel(h_ref, nw_ref, w1_ref, w2_ref, fw_ref, o_ref):
    x = h_ref[...]
    xn = x * lax.rsqrt(jnp.mean(x * x, -1, keepdims=True) + EPS) * nw_ref[...]
    a = jnp.maximum(jnp.dot(xn.astype(BF16), w1_ref[...], preferred_element_type=F32), 0.0)
    y = x + jnp.dot((a * a).astype(BF16), w2_ref[...], preferred_element_type=F32)
    o_ref[...] = y * lax.rsqrt(jnp.mean(y * y, -1, keepdims=True) + EPS) * fw_ref[...]


def _mlp_weight_specs():
    return [_const_spec((1, D_MODEL)),
            pl.BlockSpec((D_MODEL, D_FF), lambda *_: (0, 0), pipeline_mode=pl.Buffered(1)),
            pl.BlockSpec((D_FF, D_MODEL), lambda *_: (0, 0), pipeline_mode=pl.Buffered(1))]


def _mlp(h2d, norm_w, w1, w2, *, front, rows_per_batch, tm):
    t = h2d.shape[0]
    assert tm <= rows_per_batch
    kern = functools.partial(_mlp_kernel, front=front, rows_per_batch=rows_per_batch)
    return pl.pallas_call(
        kern, grid=(t // tm,),
        in_specs=[pl.BlockSpec((tm, D_MODEL), lambda i: (i, 0))] + _mlp_weight_specs(),
        out_specs=pl.BlockSpec((tm, D_MODEL), lambda i: (i, 0)),
        out_shape=jax.ShapeDtypeStruct((t, D_MODEL), F32),
        compiler_params=_params("parallel"),
        name="mlp",
    )(h2d, norm_w.reshape(1, D_MODEL), w1.astype(BF16), w2.astype(BF16))


def _mlp_final(h2d, norm_w, w1, w2, final_w, *, bsz, rows_per_batch, first_row, seq, tm):
    assert rows_per_batch % HALO == 0 and first_row % HALO == 0 and tm % HALO == 0

    def rows(b, j):
        return pl.multiple_of(b * rows_per_batch + first_row + j * tm, HALO), 0

    return pl.pallas_call(
        _mlp_final_kernel, grid=(bsz, seq // tm),
        in_specs=[pl.BlockSpec((pl.Element(tm), pl.Element(D_MODEL)), rows)]
        + _mlp_weight_specs() + [_const_spec((1, D_MODEL))],
        out_specs=pl.BlockSpec((None, tm, D_MODEL), lambda b, j: (b, j, 0)),
        out_shape=jax.ShapeDtypeStruct((bsz, seq, D_MODEL), F32),
        compiler_params=_params("parallel", "parallel"),
        name="mlp_final",
    )(h2d, norm_w.reshape(1, D_MODEL), w1.astype(BF16), w2.astype(BF16), final_w.reshape(1, D_MODEL))


def _token_tile(t, cap):
    tm = cap
    while t % tm:
        tm //= 2
    return tm


def _row_tile(lp, cap):
    k = lp // ROW_ALIGN
    best = 1
    for m in range(1, k + 1):
        if k % m == 0 and m * ROW_ALIGN <= cap:
            best = m
    return best * ROW_ALIGN


def kernel(x, meta_tokens, final_norm_w, mix_norm_w, w_in, s5_lambda_re, s5_lambda_im, s5_log_step, s5_b_re, s5_b_im, s5_c_re, s5_c_im, s5_d, s5_glu_w, s5_glu_b, ssd_conv_w, ssd_conv_b, ssd_a_log, ssd_dt_bias, ssd_d, ssd_norm_w, rwkv_mu_rkv, rwkv_mu_wag, rwkv_w0, rwkv_w1, rwkv_w2, rwkv_a0, rwkv_a1, rwkv_a2, rwkv_g1, rwkv_g2, rwkv_k_k, rwkv_k_a, rwkv_r_k, rwkv_ln_w, rwkv_ln_b, proj_a, proj_b, proj_c, w_out, mlp_norm_w, mlp_w1, mlp_w2):
    bsz, seq, d = x.shape
    assert d == D_MODEL
    length = N_META + seq
    lp = -(-length // ROW_ALIGN) * ROW_ALIGN
    front = lp - length
    t = bsz * lp
    depth = w_in.shape[0]
    meta = jnp.broadcast_to(meta_tokens[None].astype(x.dtype), (bsz, N_META, d))
    h = jnp.concatenate([jnp.zeros((bsz, front, d), x.dtype), meta, x], axis=1).reshape(t, d)
    tm = _token_tile(t, 512)
    tb = _row_tile(lp, 512)
    rwkv_group = 3 if (lp // RWKV_CHUNK) % 3 == 0 else 2
    ssd_group = 3 if (lp // SSD_CHUNK) % 3 == 0 else 1

    for i in range(depth):
        xbc, rkvx, gates, z, ua, dt_c = _in_proj(h, mix_norm_w[i], _permute_w_in(w_in[i]), tm)
        tables = _s5_tables(s5_lambda_re[i], s5_lambda_im[i], s5_log_step[i], s5_b_re[i], s5_b_im[i],
                            s5_c_re[i], s5_c_im[i])
        ys5 = _s5_mixer_core(ua.reshape(bsz, lp, S5_WIDTH), tables).reshape(t, S5_WIDTH)
        xc = _ssd_pre(xbc.reshape(bsz, lp, SSD_CONV_CH), ssd_conv_w[i], ssd_conv_b[i], front=front, tb=tb)
        ys = _ssd_scan(xc, dt_c.reshape(bsz, lp, DT_PAD), ssd_dt_bias[i], ssd_a_log[i], ssd_d[i], front=front,
                       group=ssd_group)
        ys = [y.reshape(t, SSD_WIDTH) for y in ys]
        mu = jnp.concatenate([rwkv_mu_rkv[i], rwkv_mu_wag[i]], axis=0)
        vec = jnp.stack([rwkv_k_k[i], rwkv_k_a[i], rwkv_r_k[i].reshape(RWKV_WIDTH)], axis=0)
        w0a0 = jnp.concatenate([rwkv_w0[i], rwkv_a0[i]], axis=0)
        pre = _rwkv_pre(rkvx.reshape(bsz, lp, 4 * RWKV_WIDTH), mu, vec, rwkv_g1[i], rwkv_g2[i], rwkv_w1[i],
                        rwkv_w2[i], rwkv_a1[i], rwkv_a2[i], w0a0, front=front, tb=tb)
        r, v, nkk, g, bonus, lw0, kd0, b0, lw1, kd1, b1 = pre
        ycf, ycb = _rwkv_scan(r, v, nkk, lw0, kd0, b0, lw1, kd1, b1, group=rwkv_group)
        ycf, ycb = ycf.reshape(t, RWKV_WIDTH), ycb.reshape(t, RWKV_WIDTH)
        consts = [s5_d[i].reshape(1, S5_WIDTH), s5_glu_w[i].astype(BF16), s5_glu_b[i].reshape(1, 2 * S5_WIDTH),
                  ssd_norm_w[i].reshape(1, SSD_WIDTH), jnp.stack([rwkv_ln_w[i], rwkv_ln_b[i]], axis=0),
                  proj_a[i].astype(BF16), proj_b[i].astype(BF16), proj_c[i].astype(BF16),
                  w_out[i].astype(BF16)]
        h = _merge(h, gates, ys5, ua, ys[0], ys[1], z, ycf, ycb, g.reshape(t, RWKV_WIDTH),
                   bonus.reshape(t, RWKV_WIDTH), consts, tm)
        if i < depth - 1:
            h = _mlp(h, mlp_norm_w[i], mlp_w1[i], mlp_w2[i], front=front, rows_per_batch=lp, tm=tm)
    return _mlp_final(h, mlp_norm_w[depth - 1], mlp_w1[depth - 1], mlp_w2[depth - 1], final_norm_w, bsz=bsz,
                      rows_per_batch=lp, first_row=front + N_META, seq=seq, tm=_token_tile(seq, 512))
```

```python
import functools

import jax
import jax.numpy as jnp
from jax import lax
from jax.experimental import pallas as pl
from jax.experimental.pallas import tpu as pltpu

D_MODEL = 1024
N_META = 16
EPS = 1e-6
D_FF = 4 * D_MODEL

S5_WIDTH = 256
S5_GROUP = 16
S5_GROUPS = 16
S5_STATE = 64
S5_Q = 16

SSD_WIDTH = 512
SSD_HEAD_DIM = 64
SSD_HEADS = 8
SSD_GROUPS = 2
SSD_HPG = SSD_HEADS // SSD_GROUPS
SSD_STATE = 128
SSD_CONV = 5
SSD_CHUNK = 128
SSD_CONV_CH = SSD_WIDTH + 2 * SSD_GROUPS * SSD_STATE

RWKV_WIDTH = 256
RWKV_HEAD = 64
RWKV_HEADS = 4
RWKV_LN_EPS = 64e-5
RWKV_CHUNK = 64

N_BRANCH = 3
DT_PAD = 128
ROW_ALIGN = 128
HALO = 8

F32 = jnp.float32
BF16 = jnp.bfloat16
VMEM_LIMIT = 56 * 1024 * 1024


def _mm(a, b):
    return jnp.dot(a.astype(BF16), b.astype(BF16), preferred_element_type=F32)


def _bf16_parts(a, parts):
    out = []
    rest = a
    for _ in range(parts):
        part = rest.astype(BF16)
        rest = rest - part.astype(F32)
        out.append(part)
    return out


def _mm_split(a, b, parts=2):
    bb = b.astype(BF16)
    terms = [jnp.dot(p, bb, preferred_element_type=F32) for p in _bf16_parts(a, parts)]
    return functools.reduce(lambda x, y: x + y, terms)


def _mm_cumsum(tri, a):
    t = tri.astype(BF16)
    terms = [jnp.dot(t, p, preferred_element_type=F32) for p in _bf16_parts(a, 3)]
    return functools.reduce(lambda x, y: x + y, terms)


def _mm_nt(a, b):
    return lax.dot_general(a.astype(BF16), b.astype(BF16), (((1,), (1,)), ((), ())),
                           preferred_element_type=F32)


def _iota(shape, dim):
    return lax.broadcasted_iota(jnp.int32, shape, dim)


def _params(*sem):
    return pltpu.CompilerParams(dimension_semantics=sem, vmem_limit_bytes=VMEM_LIMIT)


def _const_spec(shape):
    nd = len(shape)
    return pl.BlockSpec(shape, lambda *_: (0,) * nd)


def _in_proj_kernel(h_ref, nw_ref, w_ref, xbc_ref, rkvx_ref, gates_ref, z_ref, ua_ref, dt_ref):
    x = h_ref[...]
    xn = x * lax.rsqrt(jnp.mean(x * x, -1, keepdims=True) + EPS) * nw_ref[...]
    xb = xn.astype(BF16)
    off = 0
    for ref in (xbc_ref, rkvx_ref, gates_ref, z_ref, ua_ref, dt_ref):
        n = ref.shape[-1]
        ref[...] = jnp.dot(xb, w_ref[:, off:off + n], preferred_element_type=F32).astype(ref.dtype)
        off += n


def _in_proj(h2d, norm_w, w_perm, tm):
    t = h2d.shape[0]
    widths = (SSD_CONV_CH, 4 * RWKV_WIDTH, N_BRANCH * D_MODEL, SSD_WIDTH, S5_WIDTH, DT_PAD)
    dtypes = (F32, F32, BF16, BF16, F32, F32)
    n_all = sum(widths)
    return pl.pallas_call(
        _in_proj_kernel,
        grid=(t // tm,),
        in_specs=[pl.BlockSpec((tm, D_MODEL), lambda i: (i, 0)),
                  _const_spec((1, D_MODEL)),
                  pl.BlockSpec((D_MODEL, n_all), lambda i: (0, 0), pipeline_mode=pl.Buffered(1))],
        out_specs=[pl.BlockSpec((tm, n), lambda i: (i, 0)) for n in widths],
        out_shape=[jax.ShapeDtypeStruct((t, n), dt) for n, dt in zip(widths, dtypes)],
        compiler_params=_params("parallel"),
        name="in_proj",
    )(h2d, norm_w.reshape(1, D_MODEL), w_perm)


def _permute_w_in(w):
    o_u, o_z, o_xbc = 0, S5_WIDTH, S5_WIDTH + SSD_WIDTH
    o_dt = o_xbc + SSD_CONV_CH
    o_rkvx = o_dt + SSD_HEADS
    o_g = o_rkvx + 4 * RWKV_WIDTH
    dt_cols = jnp.pad(w[:, o_dt:o_rkvx], ((0, 0), (0, DT_PAD - SSD_HEADS)))
    return jnp.concatenate([w[:, o_xbc:o_dt], w[:, o_rkvx:o_g], w[:, o_g:], w[:, o_z:o_xbc],
                            w[:, o_u:o_z], dt_cols], axis=1).astype(BF16)


def _s5_tables(lam_re, lam_im, log_step, b_re, b_im, c_re, c_im):
    g, n, hh = S5_GROUPS, S5_STATE, S5_GROUP
    step = jnp.exp(log_step)[:, :, None]

    def power(j):
        mag = jnp.exp(lam_re * step * j)
        return mag * jnp.cos(lam_im * step * j), mag * jnp.sin(lam_im * step * j)

    ab_re, ab_im = power(1.0)
    aq_re, aq_im = power(float(S5_Q))
    den = lam_re * lam_re + lam_im * lam_im
    co_re = ((ab_re - 1.0) * lam_re + ab_im * lam_im) / den
    co_im = (ab_im * lam_re - (ab_re - 1.0) * lam_im) / den
    eb_re = co_re[..., None] * b_re - co_im[..., None] * b_im
    eb_im = co_re[..., None] * b_im + co_im[..., None] * b_re
    eye = jnp.eye(g, dtype=F32)
    b_in = jnp.stack([jnp.einsum('dgni,gk->dgikn', e, eye) for e in (eb_re, eb_im)], axis=3)
    b_in = b_in.reshape(2, g * hh, 2 * g * n).astype(BF16)
    c_out = jnp.stack([jnp.einsum('gon,gk->gnko', c, eye) for c in (c_re, -c_im)], axis=0)
    c_out = c_out.reshape(2 * g * n, g * hh).astype(BF16)
    rows = [ab_re[0], ab_im[0], ab_re[1], ab_im[1], aq_re[0], aq_im[0], aq_re[1], aq_im[1]]
    return b_in, c_out, jnp.stack([r.reshape(g * n) for r in rows], axis=0)


def _s5_kernel(u_ref, bin_ref, cout_ref, a_ref, y_ref, h_ref, bnd_ref, us_ref, ys_ref):
    q = S5_Q
    nc = u_ref.shape[1] // q
    gn = S5_GROUPS * S5_STATE
    half = S5_WIDTH // 2
    for hh in range(2):
        us_ref[hh] = u_ref[0, :, hh * half:(hh + 1) * half]

    def token_rows(ref, hh, t):
        return ref.at[hh, pl.ds(t, nc, stride=q), :]

    def advance(i):
        for d, t in ((0, i), (1, q - 1 - i)):
            ut = jnp.concatenate([token_rows(us_ref, hh, t)[...] for hh in range(2)], axis=1)
            e = jnp.dot(ut.astype(BF16), bin_ref[d], preferred_element_type=F32)
            a_re, a_im = a_ref[2 * d:2 * d + 1, :], a_ref[2 * d + 1:2 * d + 2, :]
            h_re, h_im = h_ref[2 * d], h_ref[2 * d + 1]
            h_ref[2 * d] = a_re * h_re - a_im * h_im + e[:, 0:gn]
            h_ref[2 * d + 1] = a_re * h_im + a_im * h_re + e[:, gn:2 * gn]

    h_ref[...] = jnp.zeros_like(h_ref)

    def local_step(i, carry):
        advance(i)
        return carry

    lax.fori_loop(0, q, local_step, 0)

    sub = _iota((HALO, gn), 0)

    def boundary(i, carry):
        starts = (pl.multiple_of(i * HALO, HALO), pl.multiple_of(nc - HALO - i * HALO, HALO))
        new = []
        for d in range(2):
            h_re, h_im = carry[2 * d], carry[2 * d + 1]
            a_re, a_im = a_ref[4 + 2 * d:5 + 2 * d, :], a_ref[5 + 2 * d:6 + 2 * d, :]
            s_re = h_ref[2 * d, pl.ds(starts[d], HALO), :]
            s_im = h_ref[2 * d + 1, pl.ds(starts[d], HALO), :]
            out_re = jnp.zeros((HALO, gn), F32)
            out_im = jnp.zeros((HALO, gn), F32)
            for step in range(HALO):
                rix = HALO - 1 - step if d else step
                out_re = jnp.where(sub == rix, h_re, out_re)
                out_im = jnp.where(sub == rix, h_im, out_im)
                n_re = a_re * h_re - a_im * h_im + s_re[rix:rix + 1, :]
                n_im = a_re * h_im + a_im * h_re + s_im[rix:rix + 1, :]
                h_re, h_im = n_re, n_im
            bnd_ref[2 * d, pl.ds(starts[d], HALO), :] = out_re
            bnd_ref[2 * d + 1, pl.ds(starts[d], HALO), :] = out_im
            new += [h_re, h_im]
        return tuple(new)

    zero = jnp.zeros((1, gn), F32)
    lax.fori_loop(0, nc // HALO, boundary, (zero,) * 4)

    h_ref[...] = bnd_ref[...]

    def emit(i, first):
        advance(i)
        for d, t in ((0, i), (1, q - 1 - i)):
            hcat = jnp.concatenate([h_ref[2 * d], h_ref[2 * d + 1]], axis=1).astype(BF16)
            part = jnp.dot(hcat, cout_ref[...], preferred_element_type=F32)
            for hh in range(2):
                rows = token_rows(ys_ref, hh, t)
                piece = part[:, hh * half:(hh + 1) * half]
                rows[...] = piece if first else rows[...] + piece

    def emit_first(i, carry):
        emit(i, True)
        return carry

    def emit_second(i, carry):
        emit(i, False)
        return carry

    lax.fori_loop(0, q // 2, emit_first, 0)
    lax.fori_loop(q // 2, q, emit_second, 0)
    for hh in range(2):
        y_ref[0, :, hh * half:(hh + 1) * half] = ys_ref[hh]


def _s5_mixer_core(u, tables):
    b_in, c_out, a_rows = tables
    bsz, lp, width = u.shape
    nc = lp // S5_Q
    gn = S5_GROUPS * S5_STATE
    row = pl.BlockSpec((1, lp, width), lambda b: (b, 0, 0))
    return pl.pallas_call(
        _s5_kernel, grid=(bsz,),
        in_specs=[row, _const_spec(b_in.shape), _const_spec(c_out.shape), _const_spec(a_rows.shape)],
        out_specs=row, out_shape=jax.ShapeDtypeStruct((bsz, lp, width), F32),
        scratch_shapes=[pltpu.VMEM((4, nc, gn), F32)] * 2 + [pltpu.VMEM((2, lp, width // 2), F32)] * 2,
        compiler_params=_params("parallel"), name="s5_scan")(u, b_in, c_out, a_rows)


def _ssd_pre_kernel(cur_ref, prev_ref, next_ref, cw_ref, cb_ref, xc_ref, *, front):
    j = pl.program_id(1)
    nb = pl.num_programs(1)
    tb = cur_ref.shape[1]
    ext = jnp.concatenate([prev_ref[0] * jnp.where(j > 0, 1.0, 0.0), cur_ref[0],
                           next_ref[0] * jnp.where(j < nb - 1, 1.0, 0.0)], axis=0)
    n = tb + 2 * HALO
    lo = pltpu.roll(pltpu.roll(cw_ref[0:1, :] * ext, 1, 0) + cw_ref[1:2, :] * ext, 1, 0)
    hi = pltpu.roll(pltpu.roll(cw_ref[4:5, :] * ext, n - 1, 0) + cw_ref[3:4, :] * ext, n - 1, 0)
    acc = (lo + hi + cw_ref[2:3, :] * ext)[HALO:HALO + tb] + cb_ref[...]
    valid = (j * tb + _iota((tb, 1), 0)) >= front
    xc_ref[0] = jnp.where(valid, acc * jax.nn.sigmoid(acc), 0.0).astype(BF16)


def _ssd_pre(xbc, conv_w, conv_b, *, front, tb):
    bsz, lp, ch = xbc.shape
    per = tb // HALO
    nh = lp // HALO
    cur = pl.BlockSpec((1, tb, ch), lambda b, j: (b, j, 0))
    prv = pl.BlockSpec((1, HALO, ch), lambda b, j: (b, jnp.maximum(j * per - 1, 0), 0))
    nxt = pl.BlockSpec((1, HALO, ch), lambda b, j: (b, jnp.minimum(j * per + per, nh - 1), 0))
    cw = jnp.pad(conv_w, ((0, HALO - SSD_CONV), (0, 0)))
    return pl.pallas_call(
        functools.partial(_ssd_pre_kernel, front=front), grid=(bsz, lp // tb),
        in_specs=[cur, prv, nxt, _const_spec((HALO, ch)), _const_spec((1, ch))],
        out_specs=cur, out_shape=jax.ShapeDtypeStruct((bsz, lp, ch), BF16),
        compiler_params=_params("parallel", "parallel"), name="ssd_pre",
    )(xbc, xbc, xbc, cw, conv_b.reshape(1, ch))


def _ssd_kernel(xcf_ref, dcf_ref, xcb_ref, dcb_ref, pc_ref, dskip_ref, yf_ref, yb_ref, hf_ref, hb_ref, *,
                front, group):
    c = pl.program_id(1)
    nb = pl.num_programs(1)
    q = SSD_CHUNK
    gw = SSD_HPG * SSD_HEAD_DIM

    @pl.when(c == 0)
    def _():
        hf_ref[...] = jnp.zeros_like(hf_ref)
        hb_ref[...] = jnp.zeros_like(hb_ref)

    rr, cc = _iota((q, q), 0), _iota((q, q), 1)
    rowi = _iota((q, 1), 0)
    lane_head = jnp.right_shift(_iota((q, gw), 1), 6)
    spread = jnp.right_shift(_iota((DT_PAD, SSD_WIDTH), 1), 6) == _iota((DT_PAD, SSD_WIDTH), 0)
    dirs = ((xcf_ref, dcf_ref, yf_ref, hf_ref, False, c), (xcb_ref, dcb_ref, yb_ref, hb_ref, True, nb - 1 - c))

    chains = []
    for step in range(group):
        for d, (xc_ref, dc_ref, _, _, rev, blk) in enumerate(dirs):
            gi = group - 1 - step if rev else step
            rows = slice(gi * q, (gi + 1) * q)
            valid = ((blk * group + gi) * q + rowi) >= front
            tri = (cc >= rr) if rev else (cc <= rr)
            dt = jnp.where(valid, jax.nn.softplus(dc_ref[0, rows, :] + pc_ref[2 * d:2 * d + 1, :]), 0.0)
            cs = _mm_cumsum(tri, dt * pc_ref[2 * d + 1:2 * d + 2, :])
            tot = cs[0:1, :] if rev else cs[q - 1:q, :]
            wide = _mm_split(jnp.concatenate([dt, jnp.exp(cs), jnp.exp(tot - cs)], axis=0), spread, parts=2)
            dt_e, e_cs, to_end = wide[0:q], wide[q:2 * q], wide[2 * q:3 * q]
            decay = e_cs[0:1, :] if rev else e_cs[q - 1:q, :]
            xf = xc_ref[0, rows, 0:SSD_WIDTH].astype(F32)
            xdt = xf * dt_e
            chains.append((d, rows, tri, cs, cs.T, e_cs, xdt * to_end, decay, xf, xdt))

    for g in range(SSD_GROUPS):
        lanes = slice(g * gw, (g + 1) * gw)
        b0 = SSD_WIDTH + g * SSD_STATE
        c0 = SSD_WIDTH + (SSD_GROUPS + g) * SSD_STATE
        local = []
        for d, rows, tri, cs, cs_t, e_cs, xdt_end, decay, xf, xdt in chains:
            xc_ref = dirs[d][0]
            bg = xc_ref[0, rows, b0:b0 + SSD_STATE]
            cg = xc_ref[0, rows, c0:c0 + SSD_STATE]
            cb = lax.dot_general(cg, bg, (((1,), (1,)), ((), ())), preferred_element_type=F32)
            mcat, xbd = [], []
            for jj in range(SSD_HPG):
                j = g * SSD_HPG + jj
                seg = jnp.exp(jnp.where(tri, cs[:, j:j + 1] - cs_t[j:j + 1, :], -jnp.inf))
                mcat.append((cb * seg).astype(BF16))
                xbd.append(jnp.where(lane_head == jj, xdt[:, lanes], 0.0).astype(BF16))
            y_diag = jnp.dot(jnp.concatenate(mcat, axis=1), jnp.concatenate(xbd, axis=0), preferred_element_type=F32)
            if d == 0:
                y_diag = y_diag + dskip_ref[:, lanes] * xf[:, lanes]
            st = jnp.dot(bg.astype(F32).T.astype(BF16), xdt_end[:, lanes].astype(BF16), preferred_element_type=F32)
            local.append((cg, y_diag, st))
        states = [dirs[d][3][g] for d in range(2)]
        for (d, rows, tri, cs, cs_t, e_cs, xdt_end, decay, xf, xdt), (cg, y_diag, st) in zip(chains, local):
            y_off = jnp.dot(cg, states[d].astype(BF16), preferred_element_type=F32) * e_cs[:, lanes]
            dirs[d][2][0, rows, lanes] = y_diag + y_off
            states[d] = states[d] * decay[:, lanes] + st
        for d in range(2):
            dirs[d][3][g] = states[d]


def _ssd_scan(xc, dt_c, dt_bias, a_log, d_skip, *, front, group):
    bsz, lp, ch = xc.shape
    rows = group * SSD_CHUNK
    nb = lp // rows

    def spec(width, rev):
        return pl.BlockSpec((1, rows, width), (lambda b, c: (b, nb - 1 - c, 0)) if rev else (lambda b, c: (b, c, 0)))

    aneg = -jnp.exp(a_log)
    pc = jnp.stack([jnp.pad(r, (0, DT_PAD - SSD_HEADS)) for r in (dt_bias[0], aneg[0], dt_bias[1], aneg[1])], axis=0)
    dsk = jnp.repeat(d_skip, SSD_HEAD_DIM).reshape(1, SSD_WIDTH)
    gw = SSD_HPG * SSD_HEAD_DIM
    return pl.pallas_call(
        functools.partial(_ssd_kernel, front=front, group=group), grid=(bsz, nb),
        in_specs=[spec(ch, False), spec(DT_PAD, False), spec(ch, True), spec(DT_PAD, True),
                  _const_spec(pc.shape), _const_spec(dsk.shape)],
        out_specs=[spec(SSD_WIDTH, False), spec(SSD_WIDTH, True)],
        out_shape=[jax.ShapeDtypeStruct((bsz, lp, SSD_WIDTH), F32)] * 2,
        scratch_shapes=[pltpu.VMEM((SSD_GROUPS, SSD_STATE, gw), F32)] * 2,
        compiler_params=_params("parallel", "arbitrary"), name="ssd_scan",
    )(xc, dt_c, xc, dt_c, pc, dsk)


def _head_ones():
    rr, cc = _iota((RWKV_WIDTH, RWKV_WIDTH), 0), _iota((RWKV_WIDTH, RWKV_WIDTH), 1)
    return (jnp.right_shift(rr, 6) == jnp.right_shift(cc, 6)).astype(F32)


def _rwkv_pre_kernel(cur_ref, prev_ref, next_ref, mu_ref, vec_ref, g1_ref, g2_ref, w1_ref, w2_ref, a1_ref,
                     a2_ref, w0a0_ref,
                     r_ref, v_ref, nkk_ref, g_ref, bonus_ref, lw0_ref, kd0_ref, b0_ref, lw1_ref, kd1_ref,
                     b1_ref, *, front):
    j = pl.program_id(1)
    nb = pl.num_programs(1)
    tb = cur_ref.shape[1]
    w = RWKV_WIDTH
    cur = cur_ref[0]
    prow = prev_ref[0, HALO - 1:HALO, :] * jnp.where(j > 0, 1.0, 0.0)
    nrow = next_ref[0, 0:1, :] * jnp.where(j < nb - 1, 1.0, 0.0)
    rowi = _iota((tb, 1), 0)
    prev = jnp.where(rowi == 0, prow, pltpu.roll(cur, 1, 0))
    nxt = jnp.where(rowi == tb - 1, nrow, pltpu.roll(cur, tb - 1, 0))
    shift = 0.5 * (prev + nxt) - cur
    valid = (j * tb + rowi) >= front
    r = cur[:, 0:w] + shift[:, 0:w] * mu_ref[0:1, :]
    k = jnp.where(valid, cur[:, w:2 * w] + shift[:, w:2 * w] * mu_ref[1:2, :], 0.0)
    v = jnp.where(valid, cur[:, 2 * w:3 * w] + shift[:, 2 * w:3 * w] * mu_ref[2:3, :], 0.0)
    xc, dxc = cur[:, 3 * w:], shift[:, 3 * w:]
    xw = xc + dxc * mu_ref[3:4, :]
    xa = xc + dxc * mu_ref[4:5, :]
    xg = xc + dxc * mu_ref[5:6, :]
    k_k, k_a, r_k = vec_ref[0:1, :], vec_ref[1:2, :], vec_ref[2:3, :]
    ones = _head_ones()
    g_ref[0] = _mm(jax.nn.sigmoid(_mm(xg, g1_ref[...])), g2_ref[...])
    kk = k * k_k
    kk = kk * lax.rsqrt(_mm_split(kk * kk, ones) + 1e-12)
    r_ref[0] = r.astype(r_ref.dtype)
    v_ref[0] = v.astype(v_ref.dtype)
    nkk_ref[0] = (-kk).astype(nkk_ref.dtype)
    bonus_ref[0] = _mm_split(r * k * r_k, ones) * v
    for d, (lw_ref, kd_ref, b_ref) in enumerate(((lw0_ref, kd0_ref, b0_ref), (lw1_ref, kd1_ref, b1_ref))):
        lw = w0a0_ref[d:d + 1, :] + _mm(jnp.tanh(_mm(xw, w1_ref[d])), w2_ref[d])
        w_log = -jax.nn.softplus(-lw) - 0.5
        lw_ref[0] = -jnp.exp(w_log)
        ag = jax.nn.sigmoid(w0a0_ref[2 + d:3 + d, :] + _mm(_mm(xa, a1_ref[d]), a2_ref[d]))
        kd_ref[0] = (k * (1.0 + (ag - 1.0) * k_a)).astype(kd_ref.dtype)
        b_ref[0] = (kk * ag).astype(b_ref.dtype)


def _rwkv_pre(rkvx, mu, vec, g1, g2, w1, w2, a1, a2, w0a0, *, front, tb):
    bsz, lp, _ = rkvx.shape
    per = tb // HALO
    nh = lp // HALO
    w = RWKV_WIDTH
    cur = pl.BlockSpec((1, tb, 4 * w), lambda b, j: (b, j, 0))
    prv = pl.BlockSpec((1, HALO, 4 * w), lambda b, j: (b, jnp.maximum(j * per - 1, 0), 0))
    nxt = pl.BlockSpec((1, HALO, 4 * w), lambda b, j: (b, jnp.minimum(j * per + per, nh - 1), 0))
    out = pl.BlockSpec((1, tb, w), lambda b, j: (b, j, 0))
    consts = [mu, vec, g1, g2, w1, w2, a1, a2, w0a0]
    return pl.pallas_call(
        functools.partial(_rwkv_pre_kernel, front=front), grid=(bsz, lp // tb),
        in_specs=[cur, prv, nxt] + [_const_spec(c.shape) for c in consts],
        out_specs=[out] * 11,
        out_shape=[jax.ShapeDtypeStruct((bsz, lp, w), dt) for dt in
                   (BF16, BF16, BF16, F32, F32, F32, BF16, BF16, F32, BF16, BF16)],
        compiler_params=_params("parallel", "parallel"),
        name="rwkv_pre",
    )(rkvx, rkvx, rkvx, *consts)


def _rwkv_chunk_ops(chains):
    n = RWKV_CHUNK
    w = RWKV_WIDTH
    lane_head = jnp.right_shift(_iota((n, w), 1), 6)

    def stack(x):
        return jnp.concatenate([jnp.where(lane_head == h, x, 0.0) for h in range(RWKV_HEADS)], axis=0)

    rr, cc = _iota((w, w), 0), _iota((w, w), 1)
    same = jnp.right_shift(rr, 6) == jnp.right_shift(cc, 6)
    tl, sl = jnp.bitwise_and(rr, n - 1), jnp.bitwise_and(cc, n - 1)
    eye = (rr == cc).astype(F32)
    r8, c8 = _iota((n, n), 0), _iota((n, n), 1)
    masks = {}
    for rev in sorted({ch[6] for ch in chains}):
        masks[rev] = (same & ((sl > tl) if rev else (sl < tl)), same & ((sl >= tl) if rev else (sl <= tl)),
                      (c8 >= r8) if rev else (c8 <= r8))

    pre = []
    for r, v, a, kd, b, lw, rev in chains:
        cum = _mm_cumsum(masks[rev][2], lw)
        tot = cum[0:1, :] if rev else cum[n - 1:n, :]
        p_inv = jnp.exp(-cum)
        p_end = jnp.exp(tot - cum)
        r_t = stack(r * jnp.exp(cum))
        a_t = stack(a * jnp.exp(cum - lw))
        lhs = jnp.concatenate([a_t, r_t], axis=0)
        rhs = jnp.concatenate([stack(b * p_inv), stack(kd * p_inv)], axis=0)
        pre.append((lhs, rhs, r_t, a_t, stack(b * p_end).T, stack(kd * p_end).T, stack(v), jnp.exp(tot)))

    grams = [_mm_nt(p[0], p[1]) for p in pre]
    parts = []
    for ch, gram in zip(chains, grams):
        strict, incl, _ = masks[ch[6]]
        parts.append((jnp.where(strict, gram[0:w, 0:w], 0.0), jnp.where(strict, gram[0:w, w:2 * w], 0.0),
                      jnp.where(incl, gram[w:2 * w, 0:w], 0.0), jnp.where(incl, gram[w:2 * w, w:2 * w], 0.0)))

    def same_block(shift):
        return jnp.right_shift(rr, shift) == jnp.right_shift(cc, shift)

    invs = [eye + jnp.where(same_block(1), p[0], 0.0) for p in parts]
    on_v = [_mm(jnp.concatenate([p[1], p[3], q[5]], axis=0), q[6]) for p, q in zip(parts, pre)]
    for shift in range(1, RWKV_CHUNK.bit_length() - 1):
        between = same_block(shift + 1) & (jnp.right_shift(rr, shift) != jnp.right_shift(cc, shift))
        ys = [_mm(jnp.where(between, p[0], 0.0), inv) for p, inv in zip(parts, invs)]
        invs = [inv + _mm(inv, y) for inv, y in zip(invs, ys)]
    wus = [_mm(inv, jnp.concatenate([q[3], ov[0:w]], axis=1))
           for inv, q, ov in zip(invs, pre, on_v)]
    on_wu = [_mm(jnp.concatenate([p[2], q[4]], axis=0), wu) for p, q, wu in zip(parts, pre, wus)]
    out = []
    for q, ov, ow in zip(pre, on_v, on_wu):
        rp = q[2] + ow[0:w, 0:w]
        y0 = ow[0:w, w:2 * w] + ov[w:2 * w]
        mk = eye * q[7] + ow[w:2 * w, 0:w]
        nk = ow[w:2 * w, w:2 * w] + ov[2 * w:3 * w]
        out.append((y0, rp, mk, nk))
    return out


def _rwkv_scan_kernel(rf_ref, vf_ref, af_ref, lwf_ref, kdf_ref, bf_ref,
                      rb_ref, vb_ref, ab_ref, lwb_ref, kdb_ref, bb_ref,
                      yf_ref, yb_ref, sf_ref, sb_ref, *, group):
    n = RWKV_CHUNK

    @pl.when(pl.program_id(1) == 0)
    def _():
        sf_ref[...] = jnp.zeros_like(sf_ref)
        sb_ref[...] = jnp.zeros_like(sb_ref)

    def unstack(x):
        out = x[0:n]
        for h in range(1, RWKV_HEADS):
            out = out + x[h * n:(h + 1) * n]
        return out

    dirs = (((rf_ref, vf_ref, af_ref, kdf_ref, bf_ref, lwf_ref), yf_ref, sf_ref, False),
            ((rb_ref, vb_ref, ab_ref, kdb_ref, bb_ref, lwb_ref), yb_ref, sb_ref, True))
    chains = []
    for step in range(group):
        for refs, _, _, reverse in dirs:
            gi = group - 1 - step if reverse else step
            chains.append(tuple(ref[0, gi * n:(gi + 1) * n, :].astype(F32) for ref in refs) + (reverse,))
    ops = _rwkv_chunk_ops(chains)
    states = [s_ref[...] for _, _, s_ref, _ in dirs]
    for step in range(group):
        for d, (_, y_ref, _, reverse) in enumerate(dirs):
            gi = group - 1 - step if reverse else step
            y0, rp, mk, nk = ops[step * len(dirs) + d]
            on_st = _mm(jnp.concatenate([rp, mk], axis=0), states[d])
            y_ref[0, gi * n:(gi + 1) * n, :] = unstack(y0 + on_st[0:RWKV_WIDTH])
            states[d] = on_st[RWKV_WIDTH:] + nk
    for d, (_, _, s_ref, _) in enumerate(dirs):
        s_ref[...] = states[d]


def _rwkv_scan(r, v, nkk, lw0, kd0, b0, lw1, kd1, b1, *, group):
    bsz, lp, w = r.shape
    rows = group * RWKV_CHUNK
    nb = lp // rows
    fwd = pl.BlockSpec((1, rows, w), lambda bb, c: (bb, c, 0))
    bwd = pl.BlockSpec((1, rows, w), lambda bb, c: (bb, nb - 1 - c, 0))
    return pl.pallas_call(
        functools.partial(_rwkv_scan_kernel, group=group), grid=(bsz, nb),
        in_specs=[fwd] * 6 + [bwd] * 6, out_specs=[fwd, bwd],
        out_shape=[jax.ShapeDtypeStruct((bsz, lp, w), F32)] * 2,
        scratch_shapes=[pltpu.VMEM((w, w), F32)] * 2,
        compiler_params=_params("parallel", "arbitrary"),
        name="rwkv_scan",
    )(r, v, nkk, lw0, kd0, b0, r, v, nkk, lw1, kd1, b1)


def _merge_kernel(h_ref, gates_ref, ys5_ref, ua_ref, ysf_ref, ysb_ref, z_ref, ycf_ref, ycb_ref, g_ref,
                  bonus_ref, vec_a_ref, glu_w_ref, glu_b_ref, ssd_nw_ref, ln_ref, pa_ref, pb_ref, pc_ref,
                  wo_ref, o_ref):
    ua = ua_ref[...]
    ya = jax.nn.gelu(ys5_ref[...] + vec_a_ref[...] * ua)
    zz = _mm(ya, glu_w_ref[...]) + glu_b_ref[...]
    ya = zz[:, :S5_WIDTH] * jax.nn.sigmoid(zz[:, S5_WIDTH:])
    z = z_ref[...].astype(F32)
    yb = (ysf_ref[...] + ysb_ref[...]) * (z * jax.nn.sigmoid(z))
    yb = yb * lax.rsqrt(jnp.mean(yb * yb, -1, keepdims=True) + EPS) * ssd_nw_ref[...]
    yc = ycf_ref[...] + ycb_ref[...]
    avg = _head_ones() * (1.0 / RWKV_HEAD)
    mean = _mm_split(yc, avg)
    dev = yc - mean
    var = _mm_split(dev * dev, avg)
    yc = dev * lax.rsqrt(var + RWKV_LN_EPS) * ln_ref[0:1, :] + ln_ref[1:2, :]
    yc = (yc + bonus_ref[...]) * g_ref[...]
    d = D_MODEL
    merged = (jax.nn.sigmoid(gates_ref[:, 0:d].astype(F32)) * _mm(ya, pa_ref[...])
              + jax.nn.sigmoid(gates_ref[:, d:2 * d].astype(F32)) * _mm(yb, pb_ref[...])
              + jax.nn.sigmoid(gates_ref[:, 2 * d:3 * d].astype(F32)) * _mm(yc, pc_ref[...]))
    o_ref[...] = h_ref[...] + _mm(merged, wo_ref[...])


def _merge(h2d, gates, ys5, ua, ysf, ysb, z, ycf, ycb, g, bonus, consts, tm):
    t = h2d.shape[0]
    toks = [h2d, gates, ys5, ua, ysf, ysb, z, ycf, ycb, g, bonus]
    return pl.pallas_call(
        _merge_kernel, grid=(t // tm,),
        in_specs=[pl.BlockSpec((tm, a.shape[1]), lambda i: (i, 0)) for a in toks]
        + [_const_spec(c.shape) for c in consts],
        out_specs=pl.BlockSpec((tm, D_MODEL), lambda i: (i, 0)),
        out_shape=jax.ShapeDtypeStruct((t, D_MODEL), F32),
        compiler_params=_params("parallel"),
        name="merge",
    )(*toks, *consts)


def _mlp_kernel(h_ref, nw_ref, w1_ref, w2_ref, o_ref, *, front, rows_per_batch):
    x = h_ref[...]
    tm = x.shape[0]
    xn = x * lax.rsqrt(jnp.mean(x * x, -1, keepdims=True) + EPS) * nw_ref[...]
    a = jnp.maximum(jnp.dot(xn.astype(BF16), w1_ref[...], preferred_element_type=F32), 0.0)
    y = x + jnp.dot((a * a).astype(BF16), w2_ref[...], preferred_element_type=F32)
    row = (pl.program_id(0) * tm) % rows_per_batch + _iota((tm, 1), 0)
    row = jnp.where(row >= rows_per_batch, row - rows_per_batch, row)
    o_ref[...] = jnp.where(row >= front, y, 0.0)


def _mlp_final_kernel(h_ref, nw_ref, w1_ref, w2_ref, fw_ref, o_ref):
    x = h_ref[...]
    xn = x * lax.rsqrt(jnp.mean(x * x, -1, keepdims=True) + EPS) * nw_ref[...]
    a = jnp.maximum(jnp.dot(xn.astype(BF16), w1_ref[...], preferred_element_type=F32), 0.0)
    y = x + jnp.dot((a * a).astype(BF16), w2_ref[...], preferred_element_type=F32)
    o_ref[...] = y * lax.rsqrt(jnp.mean(y * y, -1, keepdims=True) + EPS) * fw_ref[...]


def _mlp_weight_specs():
    return [_const_spec((1, D_MODEL)),
            pl.BlockSpec((D_MODEL, D_FF), lambda *_: (0, 0), pipeline_mode=pl.Buffered(1)),
            pl.BlockSpec((D_FF, D_MODEL), lambda *_: (0, 0), pipeline_mode=pl.Buffered(1))]


def _mlp(h2d, norm_w, w1, w2, *, front, rows_per_batch, tm):
    t = h2d.shape[0]
    assert tm <= rows_per_batch
    kern = functools.partial(_mlp_kernel, front=front, rows_per_batch=rows_per_batch)
    return pl.pallas_call(
        kern, grid=(t // tm,),
        in_specs=[pl.BlockSpec((tm, D_MODEL), lambda i: (i, 0))] + _mlp_weight_specs(),
        out_specs=pl.BlockSpec((tm, D_MODEL), lambda i: (i, 0)),
        out_shape=jax.ShapeDtypeStruct((t, D_MODEL), F32),
        compiler_params=_params("parallel"),
        name="mlp",
    )(h2d, norm_w.reshape(1, D_MODEL), w1.astype(BF16), w2.astype(BF16))


def _mlp_final(h2d, norm_w, w1, w2, final_w, *, bsz, rows_per_batch, first_row, seq, tm):
    assert rows_per_batch % HALO == 0 and first_row % HALO == 0 and tm % HALO == 0

    def rows(b, j):
        return pl.multiple_of(b * rows_per_batch + first_row + j * tm, HALO), 0

    return pl.pallas_call(
        _mlp_final_kernel, grid=(bsz, seq // tm),
        in_specs=[pl.BlockSpec((pl.Element(tm), pl.Element(D_MODEL)), rows)]
        + _mlp_weight_specs() + [_const_spec((1, D_MODEL))],
        out_specs=pl.BlockSpec((None, tm, D_MODEL), lambda b, j: (b, j, 0)),
        out_shape=jax.ShapeDtypeStruct((bsz, seq, D_MODEL), F32),
        compiler_params=_params("parallel", "parallel"),
        name="mlp_final",
    )(h2d, norm_w.reshape(1, D_MODEL), w1.astype(BF16), w2.astype(BF16), final_w.reshape(1, D_MODEL))


def _token_tile(t, cap):
    tm = cap
    while t % tm:
        tm //= 2
    return tm


def _row_tile(lp, cap):
    k = lp // ROW_ALIGN
    best = 1
    for m in range(1, k + 1):
        if k % m == 0 and m * ROW_ALIGN <= cap:
            best = m
    return best * ROW_ALIGN


def kernel(x, meta_tokens, final_norm_w, mix_norm_w, w_in, s5_lambda_re, s5_lambda_im, s5_log_step, s5_b_re, s5_b_im, s5_c_re, s5_c_im, s5_d, s5_glu_w, s5_glu_b, ssd_conv_w, ssd_conv_b, ssd_a_log, ssd_dt_bias, ssd_d, ssd_norm_w, rwkv_mu_rkv, rwkv_mu_wag, rwkv_w0, rwkv_w1, rwkv_w2, rwkv_a0, rwkv_a1, rwkv_a2, rwkv_g1, rwkv_g2, rwkv_k_k, rwkv_k_a, rwkv_r_k, rwkv_ln_w, rwkv_ln_b, proj_a, proj_b, proj_c, w_out, mlp_norm_w, mlp_w1, mlp_w2):
    bsz, seq, d = x.shape
    assert d == D_MODEL
    length = N_META + seq
    lp = -(-length // ROW_ALIGN) * ROW_ALIGN
    front = lp - length
    t = bsz * lp
    depth = w_in.shape[0]
    meta = jnp.broadcast_to(meta_tokens[None].astype(x.dtype), (bsz, N_META, d))
    h = jnp.concatenate([jnp.zeros((bsz, front, d), x.dtype), meta, x], axis=1).reshape(t, d)
    tm = _token_tile(t, 512)
    tb = _row_tile(lp, 512)
    rwkv_group = 3 if (lp // RWKV_CHUNK) % 3 == 0 else 2
    ssd_group = 3 if (lp // SSD_CHUNK) % 3 == 0 else 1

    for i in range(depth):
        xbc, rkvx, gates, z, ua, dt_c = _in_proj(h, mix_norm_w[i], _permute_w_in(w_in[i]), tm)
        tables = _s5_tables(s5_lambda_re[i], s5_lambda_im[i], s5_log_step[i], s5_b_re[i], s5_b_im[i],
                            s5_c_re[i], s5_c_im[i])
        ys5 = _s5_mixer_core(ua.reshape(bsz, lp, S5_WIDTH), tables).reshape(t, S5_WIDTH)
        xc = _ssd_pre(xbc.reshape(bsz, lp, SSD_CONV_CH), ssd_conv_w[i], ssd_conv_b[i], front=front, tb=tb)
        ys = _ssd_scan(xc, dt_c.reshape(bsz, lp, DT_PAD), ssd_dt_bias[i], ssd_a_log[i], ssd_d[i], front=front,
                       group=ssd_group)
        ys = [y.reshape(t, SSD_WIDTH) for y in ys]
        mu = jnp.concatenate([rwkv_mu_rkv[i], rwkv_mu_wag[i]], axis=0)
        vec = jnp.stack([rwkv_k_k[i], rwkv_k_a[i], rwkv_r_k[i].reshape(RWKV_WIDTH)], axis=0)
        w0a0 = jnp.concatenate([rwkv_w0[i], rwkv_a0[i]], axis=0)
        pre = _rwkv_pre(rkvx.reshape(bsz, lp, 4 * RWKV_WIDTH), mu, vec, rwkv_g1[i], rwkv_g2[i], rwkv_w1[i],
                        rwkv_w2[i], rwkv_a1[i], rwkv_a2[i], w0a0, front=front, tb=tb)
        r, v, nkk, g, bonus, lw0, kd0, b0, lw1, kd1, b1 = pre
        ycf, ycb = _rwkv_scan(r, v, nkk, lw0, kd0, b0, lw1, kd1, b1, group=rwkv_group)
        ycf, ycb = ycf.reshape(t, RWKV_WIDTH), ycb.reshape(t, RWKV_WIDTH)
        consts = [s5_d[i].reshape(1, S5_WIDTH), s5_glu_w[i].astype(BF16), s5_glu_b[i].reshape(1, 2 * S5_WIDTH),
                  ssd_norm_w[i].reshape(1, SSD_WIDTH), jnp.stack([rwkv_ln_w[i], rwkv_ln_b[i]], axis=0),
                  proj_a[i].astype(BF16), proj_b[i].astype(BF16), proj_c[i].astype(BF16),
                  w_out[i].astype(BF16)]
        h = _merge(h, gates, ys5, ua, ys[0], ys[1], z, ycf, ycb, g.reshape(t, RWKV_WIDTH),
                   bonus.reshape(t, RWKV_WIDTH), consts, tm)
        if i < depth - 1:
            h = _mlp(h, mlp_norm_w[i], mlp_w1[i], mlp_w2[i], front=front, rows_per_batch=lp, tm=tm)
    return _mlp_final(h, mlp_norm_w[depth - 1], mlp_w1[depth - 1], mlp_w2[depth - 1], final_norm_w, bsz=bsz,
                      rows_per_batch=lp, first_row=front + N_META, seq=seq, tm=_token_tile(seq, 512))
```
